```python
import math
import jax
import jax.numpy as jnp
from jax import lax
import numpy as np

D_MODEL = 1024
BATCH = 1
SEQ = 16384
DEPTH = 1
DEC_BATCH = 128
DEC_SEQ = 8
PAST_LEN = 8192
PAGE_SIZE = 128

H_A = 8
DK_A = 32
DV_A = 64
H_R = 8
HD_R = 64
D_DECAY_LORA = 64
D_AAA_LORA = 64
D_GATE_LORA = 128
N_GROUPS = 4
EXPERTS_PER_GROUP = 4
N_EXPERTS = N_GROUPS * EXPERTS_PER_GROUP
D_EXPERT = 512
TOP_K_IN_GROUP = 2

Q_BLOCK = 128
NORM_EPS = 1e-6
LN_X_EPS = 64e-5
NEG_INF = -1e30

W_A = H_A * DV_A
W_R = H_R * HD_R
MIX_WIDTH = W_A + W_R
C_QA = H_A * 2 * DK_A
C_A = 2 * C_QA + W_A
C_R = 3 * W_R + D_DECAY_LORA + D_AAA_LORA + D_GATE_LORA
C_IN = C_A + C_R

kernel_name = 'hymba_diffattn_rwkv7_hmoe_step'


def rms_norm(x, g):
    xf = x.astype(jnp.float32)
    y = xf * lax.rsqrt(jnp.mean(xf * xf, -1, keepdims=True) + NORM_EPS)
    return (y * g.astype(jnp.float32)).astype(x.dtype)


def ada_mod(c, w, b):
    m = jax.nn.silu(c) @ w + b
    shift, scale, gate = jnp.split(m[:, None, :], 3, axis=-1)
    return shift, scale, gate


def diff_core(q, k, v, q_pos, k_pos, lam):
    s = jnp.einsum('qhmd,khmd->hmqk', q, k).astype(jnp.float32) * (DK_A ** -0.5)
    mask = q_pos[:, None] >= k_pos[None, :]
    p = jax.nn.softmax(jnp.where(mask, s, NEG_INF), axis=-1)
    a = p[:, 0] - lam * p[:, 1]
    return jnp.einsum('hqk,khd->qhd', a, v.astype(jnp.float32))


def diff_head_norm(o, g, lam_init):
    o = o * lax.rsqrt(jnp.mean(o * o, -1, keepdims=True) + NORM_EPS) * g.astype(jnp.float32)
    return o * (1.0 - lam_init)


def diff_attn_prompt(q, k, v, lam):
    B, S = q.shape[0], q.shape[1]
    nblk = S // Q_BLOCK
    qb = jnp.moveaxis(q.reshape(B, nblk, Q_BLOCK, H_A, 2, DK_A), 1, 0)
    starts = jnp.arange(nblk, dtype=jnp.int32) * Q_BLOCK
    k_pos = jnp.arange(S, dtype=jnp.int32)

    def block(args):
        qblk, st = args
        q_pos = st + jnp.arange(Q_BLOCK, dtype=jnp.int32)
        return jax.vmap(lambda qq, kk, vv: diff_core(qq, kk, vv, q_pos, k_pos, lam))(qblk, k, v)

    o = lax.map(block, (qb, starts))
    return jnp.moveaxis(o, 0, 1).reshape(B, S, H_A, DV_A)


def diff_attn_sample(q, k_new, v_new, cache_k, cache_v, layer, page_table, lam):
    T = q.shape[1]
    past = page_table.shape[1] * PAGE_SIZE
    q_pos = past + jnp.arange(T, dtype=jnp.int32)
    k_pos = jnp.arange(past + T, dtype=jnp.int32)

    def one(args):
        qq, kn, vn, pt = args
        kp = cache_k[layer, pt].reshape(past, H_A, 2, DK_A).astype(kn.dtype)
        vp = cache_v[layer, pt].reshape(past, H_A, DV_A).astype(vn.dtype)
        kk = jnp.concatenate([kp, kn], axis=0)
        vv = jnp.concatenate([vp, vn], axis=0)
        return diff_core(qq, kk, vv, q_pos, k_pos, lam)

    return lax.map(one, (q, k_new, v_new, page_table))


def rwkv7_time_mix(zr, shift_prev, s0, mu, w0, w2, a0, a2, g2, k_k, k_a, r_k, ln_w, ln_b):
    f32 = jnp.float32
    B, T = zr.shape[0], zr.shape[1]
    z = zr.astype(f32)
    z_prev = jnp.concatenate([shift_prev.astype(f32)[:, None], z[:, :-1]], axis=1)
    z = z + (z_prev - z) * mu.astype(f32)
    o1, o2, o3 = W_R, 2 * W_R, 3 * W_R
    o4 = o3 + D_DECAY_LORA
    o5 = o4 + D_AAA_LORA
    r, k, v = z[..., :o1], z[..., o1:o2], z[..., o2:o3]
    zw, za, zg = z[..., o3:o4], z[..., o4:o5], z[..., o5:]
    w_raw = -jax.nn.softplus(-(w0.astype(f32) + jnp.tanh(zw) @ w2.astype(f32))) - 0.5
    decay = jnp.exp(-jnp.exp(w_raw))
    a = jax.nn.sigmoid(a0.astype(f32) + za @ a2.astype(f32))
    g = jax.nn.sigmoid(zg) @ g2.astype(f32)
    hs = lambda t: t.reshape(B, T, H_R, HD_R)
    r, k, v, decay, a = hs(r), hs(k), hs(v), hs(decay), hs(a)
    kk = k * k_k.astype(f32).reshape(H_R, HD_R)
    kk = kk * lax.rsqrt(jnp.sum(kk * kk, -1, keepdims=True) + 1e-12)
    k = k * (1.0 + (a - 1.0) * k_a.astype(f32).reshape(H_R, HD_R))

    def step(S, inp):
        r_t, w_t, k_t, v_t, kk_t, a_t = inp
        sa = jnp.einsum('bhvk,bhk->bhv', S, -kk_t)
        S = (S * w_t[:, :, None, :]
             + sa[..., None] * (kk_t * a_t)[:, :, None, :]
             + v_t[..., None] * k_t[:, :, None, :])
        return S, jnp.einsum('bhvk,bhk->bhv', S, r_t)

    xs = tuple(jnp.swapaxes(t, 0, 1) for t in (r, decay, k, v, kk, a))
    s_fin, y = lax.scan(step, s0.astype(f32), xs)
    y = jnp.swapaxes(y, 0, 1)
    mean = jnp.mean(y, -1, keepdims=True)
    var = jnp.mean(jnp.square(y - mean), -1, keepdims=True)
    yn = ((y - mean) * lax.rsqrt(var + LN_X_EPS)).reshape(B, T, W_R) * ln_w.astype(f32) + ln_b.astype(f32)
    bonus = (jnp.sum(r * k * r_k.astype(f32), -1, keepdims=True) * v).reshape(B, T, W_R)
    out = (yn + bonus) * g
    return out, zr[:, -1], s_fin


def hier_moe(h, wg, bg, we, be, w_gate, w_up, w_down):
    f32 = jnp.float32
    B, T, D = h.shape
    x = h.reshape(B * T, D)
    g_logits = (x @ wg).astype(f32) + bg.astype(f32)
    g_prob = jax.nn.softmax(g_logits, -1)
    g_idx = jnp.argmax(g_logits, -1)
    g_w = jnp.take_along_axis(g_prob, g_idx[:, None], -1)
    e_logits = ((x @ we).astype(f32) + be.astype(f32)).reshape(-1, N_GROUPS, EXPERTS_PER_GROUP)
    e_in = jnp.take_along_axis(e_logits, g_idx[:, None, None], 1)[:, 0]
    e_prob = jax.nn.softmax(e_in, -1)
    top_p, top_i = lax.top_k(e_prob, TOP_K_IN_GROUP)
    top_w = top_p / jnp.sum(top_p, -1, keepdims=True) * g_w
    expert_id = g_idx[:, None] * EXPERTS_PER_GROUP + top_i
    comb = jnp.einsum('nk,nke->ne', top_w, jax.nn.one_hot(expert_id, N_EXPERTS, dtype=f32))
    hg = jnp.einsum('nd,edf->nef', x, w_gate)
    hu = jnp.einsum('nd,edf->nef', x, w_up)
    act = jax.nn.silu(hg) * hu * comb[:, :, None].astype(hg.dtype)
    out = jnp.einsum('nef,efd->nd', act, w_down)
    return out.reshape(B, T, D)


def setup_inputs(seed: int = 0) -> dict:
    key = jax.random.key(seed)
    ks = iter(jax.random.split(key, 64))
    f32 = jnp.float32

    def nrm(shape, scale=1.0):
        return jax.random.normal(next(ks), shape, f32) * scale

    def unif(shape, lo, hi):
        return jax.random.uniform(next(ks), shape, f32, lo, hi)

    n_pages = PAST_LEN // PAGE_SIZE
    n_used = DEC_BATCH * n_pages
    n_pool = n_used + max(1, n_used // 4)
    page_table = jax.random.permutation(next(ks), n_pool)[:n_used].reshape(DEC_BATCH, n_pages).astype(jnp.int32)
    ada_b = jnp.concatenate([nrm((DEPTH, 2, 2 * D_MODEL), 0.02), 1.0 + nrm((DEPTH, 2, D_MODEL), 0.02)], axis=-1)
    return {
        'x_prompt': nrm((BATCH, SEQ, D_MODEL)),
        'x_sample': nrm((DEC_BATCH, DEC_SEQ, D_MODEL)),
        'c_prompt': nrm((BATCH, D_MODEL)),
        'c_sample': nrm((DEC_BATCH, D_MODEL)),
        'cache_k': nrm((DEPTH, n_pool, PAGE_SIZE, H_A, 2, DK_A)),
        'cache_v': nrm((DEPTH, n_pool, PAGE_SIZE, H_A, DV_A)),
        'state_wkv': nrm((DEPTH, DEC_BATCH, H_R, HD_R, HD_R), 0.3),
        'state_shift': nrm((DEPTH, DEC_BATCH, C_R)),
        'page_table': page_table,
        'ada_w': nrm((DEPTH, 2, D_MODEL, 3 * D_MODEL), 0.3 * D_MODEL ** -0.5),
        'ada_b': ada_b,
        'norm_mix_g': 1.0 + nrm((DEPTH, D_MODEL), 0.02),
        'norm_ffn_g': 1.0 + nrm((DEPTH, D_MODEL), 0.02),
        'w_in': nrm((DEPTH, D_MODEL, C_IN), D_MODEL ** -0.5),
        'w_out': nrm((DEPTH, MIX_WIDTH, D_MODEL), MIX_WIDTH ** -0.5),
        'lam_q1': nrm((DEPTH, DK_A), 0.1),
        'lam_k1': nrm((DEPTH, DK_A), 0.1),
        'lam_q2': nrm((DEPTH, DK_A), 0.1),
        'lam_k2': nrm((DEPTH, DK_A), 0.1),
        'subln_g': 1.0 + nrm((DEPTH, DV_A), 0.02),
        'rw_mu': unif((DEPTH, C_R), 0.0, 1.0),
        'rw_w0': unif((DEPTH, W_R), -5.0, -1.0),
        'rw_w2': nrm((DEPTH, D_DECAY_LORA, W_R), 0.5 * D_DECAY_LORA ** -0.5),
        'rw_a0': nrm((DEPTH, W_R), 0.1),
        'rw_a2': nrm((DEPTH, D_AAA_LORA, W_R), 0.5 * D_AAA_LORA ** -0.5),
        'rw_g2': nrm((DEPTH, D_GATE_LORA, W_R), D_GATE_LORA ** -0.5),
        'rw_kk': 0.85 + nrm((DEPTH, W_R), 0.05),
        'rw_ka': 1.0 + nrm((DEPTH, W_R), 0.05),
        'rw_rk': nrm((DEPTH, H_R, HD_R), 0.1),
        'rw_lnw': 1.0 + nrm((DEPTH, W_R), 0.02),
        'rw_lnb': nrm((DEPTH, W_R), 0.02),
        'moe_wg': nrm((DEPTH, D_MODEL, N_GROUPS), D_MODEL ** -0.5),
        'moe_bg': nrm((DEPTH, N_GROUPS), 0.01),
        'moe_we': nrm((DEPTH, D_MODEL, N_EXPERTS), D_MODEL ** -0.5),
        'moe_be': nrm((DEPTH, N_EXPERTS), 0.01),
        'moe_w_gate': nrm((DEPTH, N_EXPERTS, D_MODEL, D_EXPERT), D_MODEL ** -0.5),
        'moe_w_up': nrm((DEPTH, N_EXPERTS, D_MODEL, D_EXPERT), D_MODEL ** -0.5),
        'moe_w_down': nrm((DEPTH, N_EXPERTS, D_EXPERT, D_MODEL), D_EXPERT ** -0.5),
        'final_g': 1.0 + nrm((D_MODEL,), 0.02),
    }


def reference(x_prompt, x_sample, c_prompt, c_sample, cache_k, cache_v, state_wkv, state_shift,
              page_table, ada_w, ada_b, norm_mix_g, norm_ffn_g, w_in, w_out,
              lam_q1, lam_k1, lam_q2, lam_k2, subln_g,
              rw_mu, rw_w0, rw_w2, rw_a0, rw_a2, rw_g2, rw_kk, rw_ka, rw_rk, rw_lnw, rw_lnb,
              moe_wg, moe_bg, moe_we, moe_be, moe_w_gate, moe_w_up, moe_w_down, final_g):
    f32 = jnp.float32

    def layer(x, c, l, attend, s0, shift0):
        B, T = x.shape[0], x.shape[1]
        sh, sc, gt = ada_mod(c, ada_w[l, 0], ada_b[l, 0])
        h = rms_norm(x, norm_mix_g[l]) * (1.0 + sc) + sh
        z = h @ w_in[l]
        q = z[..., :C_QA].reshape(B, T, H_A, 2, DK_A)
        k = z[..., C_QA:2 * C_QA].reshape(B, T, H_A, 2, DK_A)
        v = z[..., 2 * C_QA:C_A].reshape(B, T, H_A, DV_A)
        lam_init = 0.8 - 0.6 * math.exp(-0.3 * l)
        lam = (jnp.exp(jnp.sum(lam_q1[l].astype(f32) * lam_k1[l].astype(f32)))
               - jnp.exp(jnp.sum(lam_q2[l].astype(f32) * lam_k2[l].astype(f32))) + lam_init)
        o_a = attend(q, k, v, lam, l)
        o_a = diff_head_norm(o_a, subln_g[l], lam_init).reshape(B, T, W_A)
        o_r, shift_new, s_new = rwkv7_time_mix(z[..., C_A:], shift0, s0, rw_mu[l], rw_w0[l], rw_w2[l],
                                               rw_a0[l], rw_a2[l], rw_g2[l], rw_kk[l], rw_ka[l],
                                               rw_rk[l], rw_lnw[l], rw_lnb[l])
        mix = jnp.concatenate([o_a, o_r], axis=-1).astype(x.dtype) @ w_out[l]
        x = x + (gt * mix).astype(x.dtype)
        sh, sc, gt = ada_mod(c, ada_w[l, 1], ada_b[l, 1])
        h = rms_norm(x, norm_ffn_g[l]) * (1.0 + sc) + sh
        ffn = hier_moe(h, moe_wg[l], moe_bg[l], moe_we[l], moe_be[l],
                       moe_w_gate[l], moe_w_up[l], moe_w_down[l])
        x = x + (gt * ffn).astype(x.dtype)
        return x, k, v, s_new.astype(state_wkv.dtype), shift_new.astype(state_shift.dtype)

    attend_prompt = lambda q, k, v, lam, l: diff_attn_prompt(q, k, v, lam)
    attend_sample = lambda q, k, v, lam, l: diff_attn_sample(q, k, v, cache_k, cache_v, l, page_table, lam)

    xp, xs = x_prompt, x_sample
    kp_l, vp_l, sp_l, hp_l = [], [], [], []
    ks_l, vs_l, ss_l, hs_l = [], [], [], []
    bp = x_prompt.shape[0]
    for l in range(DEPTH):
        s0p = jnp.zeros((bp, H_R, HD_R, HD_R), f32)
        sh0p = jnp.zeros((bp, C_R), x_prompt.dtype)
        xp, kp, vp, sp, hp = layer(xp, c_prompt, l, attend_prompt, s0p, sh0p)
        xs, kss, vss, sss, hss = layer(xs, c_sample, l, attend_sample, state_wkv[l], state_shift[l])
        kp_l.append(kp); vp_l.append(vp); sp_l.append(sp); hp_l.append(hp)
        ks_l.append(kss); vs_l.append(vss); ss_l.append(sss); hs_l.append(hss)

    y_prompt = rms_norm(xp, final_g)
    y_sample = rms_norm(xs, final_g)
    k_prompt = jnp.stack(kp_l)
    v_prompt = jnp.stack(vp_l)
    wkv_prompt = jnp.stack(sp_l)
    shift_prompt = jnp.stack(hp_l)
    k_sample = jnp.stack(ks_l)
    v_sample = jnp.stack(vs_l)
    wkv_sample = jnp.stack(ss_l)
    shift_sample = jnp.stack(hs_l)
    return (y_prompt, y_sample, k_prompt, v_prompt, wkv_prompt, shift_prompt,
            k_sample, v_sample, wkv_sample, shift_sample)
```

```python
import functools
import math

import jax
import jax.numpy as jnp
from jax import lax
from jax.experimental import pallas as pl
from jax.experimental.pallas import tpu as pltpu

F32 = jnp.float32
BF16 = jnp.bfloat16

D_MODEL = 1024
PAGE_SIZE = 128
H_A = 8
DK_A = 32
DV_A = 64
H_R = 8
HD_R = 64
D_DECAY_LORA = 64
D_AAA_LORA = 64
D_GATE_LORA = 128
N_GROUPS = 4
EXPERTS_PER_GROUP = 4
N_EXPERTS = N_GROUPS * EXPERTS_PER_GROUP
D_EXPERT = 512
NORM_EPS = 1e-6
LN_X_EPS = 64e-5
NEG_INF = -1e30
W_A = H_A * DV_A
W_R = H_R * HD_R
C_QA = H_A * 2 * DK_A
C_A = 2 * C_QA + W_A
C_R = 3 * W_R + D_DECAY_LORA + D_AAA_LORA + D_GATE_LORA
LAM_INIT = 0.8 - 0.6 * math.exp(-0.3 * 0)

LANES = 128
LOG2E = 1.4426950408889634
Q_SCALE = DK_A ** -0.5 * LOG2E
VMEM_LIMIT = 56 * 1024 * 1024


def _dot(a, b):
    return jnp.dot(a, b, preferred_element_type=F32)


def _dot_nt(a, b):
    return lax.dot_general(a, b, (((1,), (1,)), ((), ())), preferred_element_type=F32)


def _dot_tn(a, b):
    return lax.dot_general(a, b, (((0,), (0,)), ((), ())), preferred_element_type=F32)


def _split3(x):
    hi = x.astype(BF16)
    r1 = x - hi.astype(F32)
    mid = r1.astype(BF16)
    lo = (r1 - mid.astype(F32)).astype(BF16)
    return hi, mid, lo


def _dot_exact_rhs(x, m_bf16):
    hi, mid, lo = _split3(x)
    return _dot(hi, m_bf16) + _dot(mid, m_bf16) + _dot(lo, m_bf16)


def _dot_exact_lhs(m_bf16, x):
    hi, mid, lo = _split3(x)
    return _dot(m_bf16, hi) + _dot(m_bf16, mid) + _dot(m_bf16, lo)


def _dot_f32(a, b):
    ah = a.astype(BF16)
    al = (a - ah.astype(F32)).astype(BF16)
    bh = b.astype(BF16)
    bl = (b - bh.astype(F32)).astype(BF16)
    return _dot(ah, bh) + _dot(ah, bl) + _dot(al, bh)


def _sigmoid(x):
    return 1.0 / (1.0 + jnp.exp(-x))


def _params(*sem):
    return pltpu.CompilerParams(dimension_semantics=sem, vmem_limit_bytes=VMEM_LIMIT)


def _ada_kernel(c_ref, w_ref, b_ref, o_ref):
    c = c_ref[...]
    s = (c * _sigmoid(c)).astype(BF16)
    o_ref[0] = _dot(s, w_ref[0]) + b_ref[0]


def _ada_mod(c_all, ada_w, ada_b):
    r = c_all.shape[0]
    n_col = 3
    return pl.pallas_call(
        _ada_kernel,
        grid=(2, n_col),
        in_specs=[
            pl.BlockSpec((r, D_MODEL), lambda s, j: (0, 0)),
            pl.BlockSpec((1, D_MODEL, D_MODEL), lambda s, j: (s, 0, j)),
            pl.BlockSpec((1, 1, D_MODEL), lambda s, j: (s, 0, j)),
        ],
        out_specs=pl.BlockSpec((1, r, D_MODEL), lambda s, j: (s, 0, j)),
        out_shape=jax.ShapeDtypeStruct((2, r, 3 * D_MODEL), F32),
        compiler_params=_params("arbitrary", "arbitrary"),
        name="ada_mod",
    )(c_all, ada_w, ada_b.reshape(2, 1, 3 * D_MODEL))


def _modulated_norm(x, g, sc, sh):
    ms = jnp.mean(x * x, axis=-1, keepdims=True)
    return (x * lax.rsqrt(ms + NORM_EPS) * g) * (1.0 + sc) + sh


def _inproj_kernel(x_ref, sh_ref, sc_ref, g_ref, w_ref, wt_ref,
                   q_ref, kb_ref, vb_ref, k_ref, v_ref, zr_ref, *, transposed):
    hb = _modulated_norm(x_ref[...], g_ref[...], sc_ref[...], sh_ref[...]).astype(BF16)
    zk = _dot(hb, w_ref[:, C_QA:2 * C_QA])
    k_ref[...] = zk
    kb_ref[...] = zk.astype(BF16)
    zv = _dot(hb, w_ref[:, 2 * C_QA:C_A])
    v_ref[...] = zv
    zr_ref[...] = _dot(hb, w_ref[:, C_A:])
    if transposed:
        q_ref[...] = (_dot_nt(wt_ref[0:C_QA], hb) * Q_SCALE).astype(BF16)
        vb_ref[...] = _dot_nt(wt_ref[C_QA:C_QA + W_A], hb).astype(BF16)
    else:
        q_ref[...] = (_dot(hb, w_ref[:, 0:C_QA]) * Q_SCALE).astype(BF16)
        vb_ref[...] = zv.astype(BF16)


def _in_proj(x, sh, sc, g, w_bf, wt_bf, *, tm, transposed):
    n = x.shape[0]
    per_tok = sh.shape[0] != 1
    mod_spec = (pl.BlockSpec((tm, D_MODEL), lambda i: (i, 0)) if per_tok
                else pl.BlockSpec((1, D_MODEL), lambda i: (0, 0)))
    tok = lambda w: pl.BlockSpec((tm, w), lambda i: (i, 0))
    if transposed:
        qv_spec = pl.BlockSpec((C_QA, tm), lambda i: (0, i))
        qv_shape = jax.ShapeDtypeStruct((C_QA, n), BF16)
    else:
        qv_spec = tok(C_QA)
        qv_shape = jax.ShapeDtypeStruct((n, C_QA), BF16)
    return pl.pallas_call(
        functools.partial(_inproj_kernel, transposed=transposed),
        grid=(n // tm,),
        in_specs=[
            tok(D_MODEL), mod_spec, mod_spec,
            pl.BlockSpec((1, D_MODEL), lambda i: (0, 0)),
            pl.BlockSpec(w_bf.shape, lambda i: (0, 0)),
            pl.BlockSpec(wt_bf.shape, lambda i: (0, 0)),
        ],
        out_specs=[qv_spec, tok(C_QA), qv_spec, tok(C_QA), tok(W_A), tok(C_R)],
        out_shape=[
            qv_shape,
            jax.ShapeDtypeStruct((n, C_QA), BF16),
            qv_shape,
            jax.ShapeDtypeStruct((n, C_QA), F32),
            jax.ShapeDtypeStruct((n, W_A), F32),
            jax.ShapeDtypeStruct((n, C_R), F32),
        ],
        compiler_params=_params("arbitrary"),
        name="in_proj_t" if transposed else "in_proj",
    )(x, sh, sc, g, w_bf, wt_bf)


def _lambda(q1_ref, k1_ref, q2_ref, k2_ref):
    l1 = jnp.sum(q1_ref[...] * k1_ref[...], axis=-1, keepdims=True)
    l2 = jnp.sum(q2_ref[...] * k2_ref[...], axis=-1, keepdims=True)
    return jnp.exp(l1) - jnp.exp(l2) + LAM_INIT


def _attn_prompt_kernel(qt_ref, k_ref, vt_ref, g_ref, lq1, lk1, lq2, lk2, o_ref,
                        m_scr, l_scr, acc_scr, *, tq, tk):
    i = pl.program_id(1)
    m_scr[...] = jnp.full(m_scr.shape, NEG_INF, F32)
    l_scr[...] = jnp.zeros(l_scr.shape, F32)
    acc_scr[...] = jnp.zeros(acc_scr.shape, F32)

    qt = qt_ref[...]
    row_q = lax.broadcasted_iota(jnp.int32, (LANES, tq), 0)
    qtm = [jnp.where((row_q >= DK_A * i4) & (row_q < DK_A * (i4 + 1)), qt, jnp.zeros_like(qt))
           for i4 in range(4)]
    row_v = lax.broadcasted_iota(jnp.int32, (LANES, tk), 0)
    row_o = lax.broadcasted_iota(jnp.int32, (LANES, tq), 0)
    first_head = row_o < DV_A

    def step(j, masked):
        start = pl.multiple_of(j * tk, tk)
        kblk = k_ref[pl.ds(start, tk), :]
        vtblk = vt_ref[:, pl.ds(start, tk)]
        if masked:
            kpos = lax.broadcasted_iota(jnp.int32, (tk, tq), 0)
            qpos = lax.broadcasted_iota(jnp.int32, (tk, tq), 1)
            visible = kpos <= qpos
        pts, alphas = [], []
        for i4 in range(4):
            st = _dot(kblk, qtm[i4])
            if masked:
                st = jnp.where(visible, st, NEG_INF)
            m_prev = m_scr[i4]
            m_new = jnp.maximum(m_prev, jnp.max(st, axis=0, keepdims=True))
            alpha = jnp.exp2(m_prev - m_new)
            pt = jnp.exp2(st - m_new)
            l_scr[i4] = alpha * l_scr[i4] + jnp.sum(pt, axis=0, keepdims=True)
            m_scr[i4] = m_new
            pts.append(pt.astype(BF16))
            alphas.append(alpha)
        vbd = jnp.concatenate(
            [jnp.where(row_v < DV_A, vtblk, jnp.zeros_like(vtblk)),
             jnp.where(row_v >= DV_A, vtblk, jnp.zeros_like(vtblk))], axis=1)
        for m in range(2):
            pcat = jnp.concatenate([pts[m], pts[2 + m]], axis=0)
            alpha_m = jnp.where(first_head, alphas[m], alphas[2 + m])
            acc_scr[m] = acc_scr[m] * alpha_m + _dot(vbd, pcat)

    def body(j, carry):
        step(j, False)
        return carry

    lax.fori_loop(0, i * (tq // tk), body, 0)
    for jj in range(tq // tk):
        step(i * (tq // tk) + jj, True)

    lam = _lambda(lq1, lk1, lq2, lk2)
    o0 = acc_scr[0] / jnp.where(first_head, l_scr[0], l_scr[2])
    o1 = acc_scr[1] / jnp.where(first_head, l_scr[1], l_scr[3])
    o = (o0 - lam * o1).reshape(2, DV_A, tq)
    ms = jnp.mean(o * o, axis=1, keepdims=True)
    o = o * lax.rsqrt(ms + NORM_EPS)
    o = o.reshape(LANES, tq) * g_ref[:, 0:1] * (1.0 - LAM_INIT)
    o_ref[...] = o.astype(o_ref.dtype)


def _attn_prompt(qt, k_bf, vt, g_col, lams, *, tq):
    s = qt.shape[1]
    tk = tq
    lam_spec = pl.BlockSpec((1, DK_A), lambda p, i: (0, 0))
    return pl.pallas_call(
        functools.partial(_attn_prompt_kernel, tq=tq, tk=tk),
        grid=(H_A // 2, s // tq),
        in_specs=[
            pl.BlockSpec((LANES, tq), lambda p, i: (p, i)),
            pl.BlockSpec((s, LANES), lambda p, i: (0, p)),
            pl.BlockSpec((LANES, s), lambda p, i: (p, 0)),
            pl.BlockSpec((LANES, LANES), lambda p, i: (0, 0)),
            lam_spec, lam_spec, lam_spec, lam_spec,
        ],
        out_specs=pl.BlockSpec((LANES, tq), lambda p, i: (p, i)),
        out_shape=jax.ShapeDtypeStruct((W_A, s), BF16),
        scratch_shapes=[
            pltpu.VMEM((4, 1, tq), F32),
            pltpu.VMEM((4, 1, tq), F32),
            pltpu.VMEM((2, LANES, tq), F32),
        ],
        compiler_params=_params("arbitrary", "arbitrary"),
        name="attn_prompt",
    )(qt, k_bf, vt, g_col, *lams)


PAGES_PER_STEP = 8


def _attn_sample_kernel(pt_ref, qbd_ref, kn_ref, vn_ref, g_ref, seg_ref, lq1, lk1, lq2, lk2, *rest,
                        n_steps, t_new):
    k_refs = rest[:PAGES_PER_STEP]
    v_refs = rest[PAGES_PER_STEP:2 * PAGES_PER_STEP]
    o_ref, m_scr, l_scr, acc_scr = rest[2 * PAGES_PER_STEP:]
    c = pl.program_id(1)

    @pl.when(c == 0)
    def _():
        m_scr[...] = jnp.full(m_scr.shape, NEG_INF, F32)
        l_scr[...] = jnp.zeros(l_scr.shape, F32)
        acc_scr[...] = jnp.zeros(acc_scr.shape, F32)

    qbd = qbd_ref[0]

    def update(kb, vb, visible):
        st = _dot(kb, qbd)
        if visible is not None:
            st = jnp.where(visible, st, NEG_INF)
        m_prev = m_scr[...]
        m_new = jnp.maximum(m_prev, jnp.max(st, axis=0, keepdims=True))
        alpha = jnp.exp2(m_prev - m_new)
        pt = jnp.exp2(st - m_new)
        l_scr[...] = alpha * l_scr[...] + jnp.sum(pt, axis=0, keepdims=True)
        m_scr[...] = m_new
        alpha_col = jnp.transpose(jnp.broadcast_to(alpha, (LANES, LANES)))[:, 0:1]
        acc_scr[...] = acc_scr[...] * alpha_col + _dot_tn(pt.astype(BF16), vb)

    kb = jnp.concatenate([r[0] for r in k_refs], axis=0).astype(BF16)
    vb = jnp.concatenate([r[0] for r in v_refs], axis=0).astype(BF16)
    update(kb, vb, None)

    @pl.when(c == n_steps - 1)
    def _():
        pad = jnp.zeros((16 - t_new, C_QA), F32)
        kn = jnp.concatenate([kn_ref[0], pad], axis=0).astype(BF16)
        vn = jnp.concatenate([vn_ref[0], pad], axis=0).astype(BF16)
        jrow = lax.broadcasted_iota(jnp.int32, (16, LANES), 0)
        tcol = lax.broadcasted_iota(jnp.int32, (16, LANES), 1) % t_new
        update(kn, vn, jrow <= tcol)

        lam = _lambda(lq1, lk1, lq2, lk2)
        l_col = jnp.transpose(jnp.broadcast_to(l_scr[...], (LANES, LANES)))[:, 0:1]
        accn = acc_scr[...] / l_col
        lane_head = lax.broadcasted_iota(jnp.int32, (t_new, W_A), 1) // DV_A
        outs = []
        for m in range(2):
            o_m = jnp.zeros((t_new, W_A), F32)
            for h in range(H_A):
                r0 = h * 2 * t_new + m * t_new
                o_m = jnp.where(lane_head == h, accn[r0:r0 + t_new, :], o_m)
            outs.append(o_m)
        o = outs[0] - lam * outs[1]
        ms = _dot_exact_rhs(o * o, seg_ref[...]) * (1.0 / DV_A)
        o_ref[0] = o * lax.rsqrt(ms + NORM_EPS) * g_ref[...] * (1.0 - LAM_INIT)


def _attn_sample(page_table, qbd, k_new, v_new, cache_k, cache_v, g_row, seg, lams):
    b, t_new = k_new.shape[0], k_new.shape[1]
    n_pages = page_table.shape[1]
    n_steps = n_pages // PAGES_PER_STEP
    pt_flat = page_table.reshape(-1)

    def page_spec(j):
        return pl.BlockSpec(
            (1, PAGE_SIZE, C_QA),
            lambda bi, c, pt: (pt[bi * n_pages + c * PAGES_PER_STEP + j], 0, 0))

    lam_spec = pl.BlockSpec((1, DK_A), lambda bi, c, pt: (0, 0))
    seq = lambda shp: pl.BlockSpec(shp, lambda bi, c, pt: (bi, 0, 0))
    grid_spec = pltpu.PrefetchScalarGridSpec(
        num_scalar_prefetch=1,
        grid=(b, n_steps),
        in_specs=[
            seq((1, C_QA, LANES)), seq((1, t_new, C_QA)), seq((1, t_new, W_A)),
            pl.BlockSpec((1, W_A), lambda bi, c, pt: (0, 0)),
            pl.BlockSpec((W_A, W_A), lambda bi, c, pt: (0, 0)),
            lam_spec, lam_spec, lam_spec, lam_spec,
        ] + [page_spec(j) for j in range(PAGES_PER_STEP)] * 2,
        out_specs=seq((1, t_new, W_A)),
        scratch_shapes=[
            pltpu.VMEM((1, LANES), F32),
            pltpu.VMEM((1, LANES), F32),
            pltpu.VMEM((LANES, W_A), F32),
        ],
    )
    return pl.pallas_call(
        functools.partial(_attn_sample_kernel, n_steps=n_steps, t_new=t_new),
        grid_spec=grid_spec,
        out_shape=jax.ShapeDtypeStruct((b, t_new, W_A), F32),
        compiler_params=_params("arbitrary", "arbitrary"),
        name="attn_sample",
    )(pt_flat, qbd, k_new, v_new, g_row, seg, *lams,
      *([cache_k] * PAGES_PER_STEP), *([cache_v] * PAGES_PER_STEP))


def _rwkv_prep_kernel(zr_ref, first_ref, mu_ref, w0_ref, w2_ref, a0_ref, a2_ref, g2_ref,
                      kk_ref, ka_ref, rk_ref, seg_ref,
                      r_o, lw_o, k_o, v_o, kn_o, b_o, g_o, bonus_o, *, period):
    z = zr_ref[...]
    row = lax.broadcasted_iota(jnp.int32, z.shape, 0)
    first = first_ref[...] if first_ref.shape[0] == z.shape[0] else first_ref[0:1, :]
    z_prev = jnp.where((row & (period - 1)) == 0, first, pltpu.roll(z, 1, 0))
    z = z + (z_prev - z) * mu_ref[...]
    o1, o2, o3 = W_R, 2 * W_R, 3 * W_R
    o5 = o3 + D_DECAY_LORA + D_AAA_LORA
    r, k, v = z[:, :o1], z[:, o1:o2], z[:, o2:o3]
    zwa = z[:, o3:o5]
    zg = z[:, o5:]
    w_pre = w0_ref[...] + _dot_f32(jnp.tanh(zwa), w2_ref[...])
    w_raw = -(jnp.maximum(-w_pre, 0.0) + jnp.log(1.0 + jnp.exp(-jnp.abs(w_pre)))) - 0.5
    lw_o[...] = -jnp.exp(w_raw)
    a = _sigmoid(a0_ref[...] + _dot(zwa.astype(BF16), a2_ref[...]))
    g_o[...] = _dot(_sigmoid(zg).astype(BF16), g2_ref[...])
    kk = k * kk_ref[...]
    ssq = _dot_exact_rhs(kk * kk, seg_ref[...])
    kn = kk * lax.rsqrt(ssq + 1e-12)
    k2 = k * (1.0 + (a - 1.0) * ka_ref[...])
    rk = _dot_exact_rhs(r * k2 * rk_ref[...], seg_ref[...])
    r_o[...] = r
    k_o[...] = k2
    v_o[...] = v
    kn_o[...] = kn
    b_o[...] = kn * a
    bonus_o[...] = rk * v


def _rwkv_prep(zr, first, period, tm, p):
    n = zr.shape[0]
    assert period & (period - 1) == 0
    tok = lambda w: pl.BlockSpec((tm, w), lambda i: (i, 0))
    full = lambda a: pl.BlockSpec(a.shape, lambda i: (0,) * a.ndim)
    first_spec = tok(C_R) if first.shape[0] == n else pl.BlockSpec((8, C_R), lambda i: (i, 0))
    consts = [p["mu"], p["w0"], p["w2"], p["a0"], p["a2"], p["g2"], p["kk"], p["ka"], p["rk"], p["seg"]]
    return pl.pallas_call(
        functools.partial(_rwkv_prep_kernel, period=period),
        grid=(n // tm,),
        in_specs=[tok(C_R), first_spec] + [full(a) for a in consts],
        out_specs=[tok(W_R)] * 8,
        out_shape=[jax.ShapeDtypeStruct((n, W_R), F32)] * 8,
        compiler_params=_params("arbitrary"),
        name="rwkv_prep",
    )(zr, first, *consts)


SUPER = 256


def _rwkv_scan_kernel(r_ref, lw_ref, k_ref, v_ref, kn_ref, b_ref, g_ref, bonus_ref,
                      lnw_ref, lnb_ref, avg_ref, s_in_ref, o_ref, s_out_ref, *, chunk, carry):
    n_blk = SUPER // chunk
    shift = chunk.bit_length() - 1
    ri = lax.broadcasted_iota(jnp.int32, (SUPER, SUPER), 0)
    ci = lax.broadcasted_iota(jnp.int32, (SUPER, SUPER), 1)
    same = (ri >> shift) == (ci >> shift)
    strict = same & (ci < ri)
    incl = same & (ci <= ri)
    m_incl = jnp.where(incl, 1.0, 0.0).astype(BF16)
    eye = jnp.where(ri == ci, 1.0, 0.0).astype(F32)

    if carry:
        @pl.when(pl.program_id(0) == 0)
        def _():
            s_out_ref[...] = s_in_ref[...]

    lw = lw_ref[...]
    cum = _dot_exact_lhs(m_incl, lw)
    gi = jnp.exp(cum)
    a_t = -kn_ref[...] * jnp.exp(cum - lw)
    ginv = jnp.exp(-cum)
    b_t = b_ref[...] * ginv
    k_t = k_ref[...] * ginv
    r_t = r_ref[...] * gi
    v_all = v_ref[...]

    lane = lax.broadcasted_iota(jnp.int32, (SUPER, LANES), 1)
    bd_r = lax.broadcasted_iota(jnp.int32, (LANES, LANES), 0) // HD_R
    bd_c = lax.broadcasted_iota(jnp.int32, (LANES, LANES), 1) // HD_R
    bd = bd_r == bd_c

    for p in range(H_R // 2):
        ls = slice(p * LANES, (p + 1) * LANES)
        a_p, b_p, k_p, r_p, v_p = a_t[:, ls], b_t[:, ls], k_t[:, ls], r_t[:, ls], v_all[:, ls]
        b_pb, k_pb = b_p.astype(BF16), k_p.astype(BF16)
        a2_p = jnp.zeros((SUPER, LANES), F32)
        p0_p = jnp.zeros((SUPER, LANES), F32)
        y0_p = jnp.zeros((SUPER, LANES), F32)
        wrbs = []
        for hh in range(2):
            hm = (lane // HD_R) == hh
            a_m = jnp.where(hm, a_p, 0.0)
            a_mb = a_m.astype(BF16)
            r_mb = jnp.where(hm, r_p, 0.0).astype(BF16)
            v_mb = jnp.where(hm, v_p, 0.0).astype(BF16)
            l_ab = jnp.where(strict, _dot_nt(a_mb, b_pb), 0.0)
            l_ak = jnp.where(strict, _dot_nt(a_mb, k_pb), 0.0)
            w_rb = jnp.where(incl, _dot_nt(r_mb, b_pb), 0.0)
            w_rk = jnp.where(incl, _dot_nt(r_mb, k_pb), 0.0)
            t_m = l_ab
            l_pow = l_ab
            for _ in range(shift - 1):
                lb = l_pow.astype(BF16)
                l_pow = _dot(lb, lb)
                t_m = t_m + l_pow + _dot(t_m.astype(BF16), l_pow.astype(BF16))
            t_mb = t_m.astype(BF16)
            a2_p = a2_p + a_m + _dot(t_mb, a_mb)
            u0 = _dot(l_ak.astype(BF16), v_mb)
            p0_p = p0_p + u0 + _dot(t_mb, u0.astype(BF16))
            y0_p = y0_p + _dot(w_rk.astype(BF16), v_mb)
            wrbs.append(w_rb.astype(BF16))

        p_rows, yr_rows = [], []
        if carry:
            s = s_out_ref[p]
        for c in range(n_blk):
            rows = slice(c * chunk, (c + 1) * chunk)
            if not carry:
                s = s_in_ref[c, p]
            x = jnp.concatenate([a2_p[rows], r_p[rows]], axis=0).astype(BF16)
            pr = _dot_nt(x, s.astype(BF16))
            p_c = pr[:chunk] + p0_p[rows]
            lhs = jnp.concatenate([p_c, v_p[rows]], axis=0).astype(BF16)
            rhs = jnp.concatenate([b_p[rows], k_p[rows]], axis=0).astype(BF16)
            upd = _dot_tn(lhs, rhs)
            g_end = gi[(c + 1) * chunk - 1:(c + 1) * chunk, ls]
            s = (s + jnp.where(bd, upd, 0.0)) * g_end
            if not carry:
                s_out_ref[c, p] = s
            p_rows.append(p_c)
            yr_rows.append(pr[chunk:])
        if carry:
            s_out_ref[p] = s
        p_full = jnp.concatenate(p_rows, axis=0)
        y = jnp.concatenate(yr_rows, axis=0) + y0_p
        for hh in range(2):
            hm = (lane // HD_R) == hh
            y = y + _dot(wrbs[hh], jnp.where(hm, p_full, 0.0).astype(BF16))
        mean = _dot_exact_rhs(y, avg_ref[...])
        yc = y - mean
        var = _dot_exact_rhs(yc * yc, avg_ref[...])
        yn = yc * lax.rsqrt(var + LN_X_EPS) * lnw_ref[:, ls] + lnb_ref[:, ls]
        o_ref[:, ls] = ((yn + bonus_ref[:, ls]) * g_ref[:, ls]).astype(o_ref.dtype)
    del eye


def _rwkv_scan(prep, lnw, lnb, avg, s_in, *, chunk, carry):
    n = prep[0].shape[0]
    n_blk = SUPER // chunk
    tok = pl.BlockSpec((SUPER, W_R), lambda i: (i, 0))
    row = pl.BlockSpec((1, W_R), lambda i: (0, 0))
    if carry:
        s_spec = pl.BlockSpec(s_in.shape, lambda i: (0, 0, 0))
    else:
        s_spec = pl.BlockSpec((n_blk,) + s_in.shape[1:], lambda i: (i, 0, 0, 0))
    return pl.pallas_call(
        functools.partial(_rwkv_scan_kernel, chunk=chunk, carry=carry),
        grid=(n // SUPER,),
        in_specs=[tok] * 8 + [row, row, pl.BlockSpec((LANES, LANES), lambda i: (0, 0)), s_spec],
        out_specs=[tok, s_spec],
        out_shape=[jax.ShapeDtypeStruct((n, W_R), BF16), jax.ShapeDtypeStruct(s_in.shape, F32)],
        compiler_params=_params("arbitrary"),
        name="rwkv_scan_carry" if carry else "rwkv_scan_seq",
    )(*prep, lnw, lnb, avg, s_in)


ROUTE_W = 128
EXPERT_LANE0 = 8


def _route(logits):
    lane = lax.broadcasted_iota(jnp.int32, logits.shape, 1)
    big = jnp.int32(1 << 20)
    g_log = jnp.where(lane < N_GROUPS, logits, -jnp.inf)
    g_max = jnp.max(g_log, axis=-1, keepdims=True)
    g_idx = jnp.min(jnp.where(g_log == g_max, lane, big), axis=-1, keepdims=True)
    g_w = 1.0 / jnp.sum(jnp.exp(g_log - g_max), axis=-1, keepdims=True)
    e_lo = EXPERT_LANE0 + g_idx * EXPERTS_PER_GROUP
    e_log = jnp.where((lane >= e_lo) & (lane < e_lo + EXPERTS_PER_GROUP), logits, -jnp.inf)
    e_max = jnp.max(e_log, axis=-1, keepdims=True)
    i1 = jnp.min(jnp.where(e_log == e_max, lane, big), axis=-1, keepdims=True)
    e_log2 = jnp.where(lane == i1, -jnp.inf, e_log)
    e_max2 = jnp.max(e_log2, axis=-1, keepdims=True)
    i2 = jnp.min(jnp.where(e_log2 == e_max2, lane, big), axis=-1, keepdims=True)
    p2 = jnp.exp(e_max2 - e_max)
    w1 = g_w / (1.0 + p2)
    w2 = g_w * p2 / (1.0 + p2)
    return jnp.where(lane == i1, w1, 0.0) + jnp.where(lane == i2, w2, 0.0)


def _ffn_kernel(x_ref, oa_ref, or_ref, gt1_ref, sh_ref, sc_ref, gt2_ref, g2_ref, gf_ref,
                wo_ref, wr_ref, br_ref, wg_ref, wu_ref, wd_ref, y_ref,
                x1_scr, h_scr, comb_scr, acc_scr, *, oa_transposed):
    e = pl.program_id(1)

    @pl.when(e == 0)
    def _():
        if oa_transposed:
            mix = _dot_tn(oa_ref[...], wo_ref[0:W_A])
        else:
            mix = _dot(oa_ref[...].astype(BF16), wo_ref[0:W_A])
        mix = mix + _dot(or_ref[...], wo_ref[W_A:])
        x1 = x_ref[...] + gt1_ref[...] * mix
        x1_scr[...] = x1
        h = _modulated_norm(x1, g2_ref[...], sc_ref[...], sh_ref[...])
        h_scr[...] = h.astype(BF16)
        comb_scr[...] = _route(_dot_f32(h, wr_ref[...]) + br_ref[...])
        acc_scr[...] = jnp.zeros(acc_scr.shape, F32)

    hb = h_scr[...]
    hg = _dot(hb, wg_ref[0])
    hu = _dot(hb, wu_ref[0])
    lane = lax.broadcasted_iota(jnp.int32, comb_scr.shape, 1)
    comb_e = jnp.sum(jnp.where(lane == EXPERT_LANE0 + e, comb_scr[...], 0.0), axis=-1, keepdims=True)
    act = (hg * _sigmoid(hg)) * hu * comb_e
    acc_scr[...] += _dot(act.astype(BF16), wd_ref[0])

    @pl.when(e == N_EXPERTS - 1)
    def _():
        x2 = x1_scr[...] + gt2_ref[...] * acc_scr[...]
        ms = jnp.mean(x2 * x2, axis=-1, keepdims=True)
        y_ref[...] = x2 * lax.rsqrt(ms + NORM_EPS) * gf_ref[...]


def _ffn(x, oa, o_r, mods, g2, gf, wo_bf, wr, br, wg_bf, wu_bf, wd_bf, *, tm, oa_transposed):
    n = x.shape[0]
    per_tok = mods[0].shape[0] != 1
    mod_spec = (pl.BlockSpec((tm, D_MODEL), lambda i, e: (i, 0)) if per_tok
                else pl.BlockSpec((1, D_MODEL), lambda i, e: (0, 0)))
    tok = lambda w: pl.BlockSpec((tm, w), lambda i, e: (i, 0))
    oa_spec = pl.BlockSpec((W_A, tm), lambda i, e: (0, i)) if oa_transposed else tok(W_A)
    row = pl.BlockSpec((1, D_MODEL), lambda i, e: (0, 0))
    return pl.pallas_call(
        functools.partial(_ffn_kernel, oa_transposed=oa_transposed),
        grid=(n // tm, N_EXPERTS),
        in_specs=[
            tok(D_MODEL), oa_spec, tok(W_R), mod_spec, mod_spec, mod_spec, mod_spec, row, row,
            pl.BlockSpec(wo_bf.shape, lambda i, e: (0, 0)),
            pl.BlockSpec(wr.shape, lambda i, e: (0, 0)),
            pl.BlockSpec((1, ROUTE_W), lambda i, e: (0, 0)),
            pl.BlockSpec((1, D_MODEL, D_EXPERT), lambda i, e: (e, 0, 0)),
            pl.BlockSpec((1, D_MODEL, D_EXPERT), lambda i, e: (e, 0, 0)),
            pl.BlockSpec((1, D_EXPERT, D_MODEL), lambda i, e: (e, 0, 0)),
        ],
        out_specs=tok(D_MODEL),
        out_shape=jax.ShapeDtypeStruct((n, D_MODEL), F32),
        scratch_shapes=[
            pltpu.VMEM((tm, D_MODEL), F32),
            pltpu.VMEM((tm, D_MODEL), BF16),
            pltpu.VMEM((tm, ROUTE_W), F32),
            pltpu.VMEM((tm, D_MODEL), F32),
        ],
        compiler_params=_params("arbitrary", "arbitrary"),
        name="ffn_t" if oa_transposed else "ffn",
    )(x, oa, o_r, *mods, g2, gf, wo_bf, wr, br, wg_bf, wu_bf, wd_bf)


def _block_diag_state(s):
    b = s.shape[0]
    s = s.reshape(b, H_R // 2, 2, HD_R, HD_R)
    z = jnp.zeros_like(s[:, :, 0])
    top = jnp.concatenate([s[:, :, 0], z], axis=-1)
    bot = jnp.concatenate([z, s[:, :, 1]], axis=-1)
    return jnp.concatenate([top, bot], axis=-2)


def _diag_state(sbd):
    b = sbd.shape[0]
    h0 = sbd[:, :, :HD_R, :HD_R]
    h1 = sbd[:, :, HD_R:, HD_R:]
    return jnp.stack([h0, h1], axis=2).reshape(b, H_R, HD_R, HD_R)


def kernel(x_prompt, x_sample, c_prompt, c_sample, cache_k, cache_v, state_wkv, state_shift, page_table, ada_w, ada_b, norm_mix_g, norm_ffn_g, w_in, w_out, lam_q1, lam_k1, lam_q2, lam_k2, subln_g, rw_mu, rw_w0, rw_w2, rw_a0, rw_a2, rw_g2, rw_kk, rw_ka, rw_rk, rw_lnw, rw_lnb, moe_wg, moe_bg, moe_we, moe_be, moe_w_gate, moe_w_up, moe_w_down, final_g):
    bp, sp, _ = x_prompt.shape
    bs, ts, _ = x_sample.shape
    assert bp == 1 and ada_w.shape[0] == 1, "single prompt row and single layer only"
    n_p, n_s = bp * sp, bs * ts
    l = 0

    w_bf = w_in[l].astype(BF16)
    wt_bf = jnp.concatenate([w_in[l][:, :C_QA].T, w_in[l][:, 2 * C_QA:C_A].T], axis=0).astype(BF16)
    wo_bf = w_out[l].astype(BF16)
    lams = [a[l].reshape(1, DK_A) for a in (lam_q1, lam_k1, lam_q2, lam_k2)]
    g_col = jnp.broadcast_to(jnp.tile(subln_g[l], 2)[:, None], (LANES, LANES))
    g_row = jnp.tile(subln_g[l], H_A).reshape(1, W_A)
    head_of = jnp.arange(W_R) // HD_R
    seg = (head_of[:, None] == head_of[None, :]).astype(BF16)
    avg = (seg[:LANES, :LANES].astype(F32) / HD_R).astype(BF16)
    zpad = jnp.zeros((D_DECAY_LORA, W_R), F32)
    rw = {
        "mu": rw_mu[l].reshape(1, C_R), "w0": rw_w0[l].reshape(1, W_R),
        "w2": jnp.concatenate([rw_w2[l], zpad], axis=0),
        "a0": rw_a0[l].reshape(1, W_R),
        "a2": jnp.concatenate([zpad, rw_a2[l]], axis=0).astype(BF16),
        "g2": rw_g2[l].astype(BF16),
        "kk": rw_kk[l].reshape(1, W_R), "ka": rw_ka[l].reshape(1, W_R),
        "rk": rw_rk[l].reshape(1, W_R), "seg": seg,
    }
    lnw, lnb = rw_lnw[l].reshape(1, W_R), rw_lnb[l].reshape(1, W_R)
    wr = jnp.zeros((D_MODEL, ROUTE_W), F32)
    wr = wr.at[:, :N_GROUPS].set(moe_wg[l]).at[:, EXPERT_LANE0:EXPERT_LANE0 + N_EXPERTS].set(moe_we[l])
    br = jnp.zeros((1, ROUTE_W), F32)
    br = br.at[0, :N_GROUPS].set(moe_bg[l]).at[0, EXPERT_LANE0:EXPERT_LANE0 + N_EXPERTS].set(moe_be[l])
    wg_bf, wu_bf, wd_bf = (a[l].astype(BF16) for a in (moe_w_gate, moe_w_up, moe_w_down))
    g1 = norm_mix_g[l].reshape(1, D_MODEL)
    g2 = norm_ffn_g[l].reshape(1, D_MODEL)
    gf = final_g.reshape(1, D_MODEL)

    r_pad = -(bs + bp) % 8
    c_all = jnp.concatenate([c_sample, c_prompt, jnp.zeros((r_pad, D_MODEL), F32)], axis=0)
    mods = _ada_mod(c_all, ada_w[l].astype(BF16), ada_b[l])

    def mod_rows(sub, part, prompt):
        m = mods[sub, :, part * D_MODEL:(part + 1) * D_MODEL]
        if prompt:
            return m[bs:bs + 1]
        return jnp.repeat(m[:bs], ts, axis=0)

    xp = x_prompt.reshape(n_p, D_MODEL)
    qt, kb, vt, k_p, v_p, zr_p = _in_proj(
        xp, mod_rows(0, 0, True), mod_rows(0, 1, True), g1, w_bf, wt_bf, tm=512, transposed=True)
    oa_t = _attn_prompt(qt, kb, vt, g_col, lams, tq=512)

    tm_r = 256
    halo = jnp.concatenate([jnp.zeros((1, C_R), F32), zr_p[tm_r - 1:n_p - 1:tm_r]], axis=0)
    first_p = jnp.zeros((n_p // tm_r, 8, C_R), F32).at[:, 0].set(halo).reshape(n_p // tm_r * 8, C_R)
    prep_p = _rwkv_prep(zr_p, first_p, tm_r, tm_r, rw)
    or_p, s_p = _rwkv_scan(prep_p, lnw, lnb, avg, jnp.zeros((H_R // 2, LANES, LANES), F32),
                           chunk=16, carry=True)
    mods_p = (mod_rows(0, 2, True), mod_rows(1, 0, True), mod_rows(1, 1, True), mod_rows(1, 2, True))
    y_p = _ffn(xp, oa_t, or_p, mods_p, g2, gf, wo_bf, wr, br, wg_bf, wu_bf, wd_bf,
               tm=512, oa_transposed=True)

    xs = x_sample.reshape(n_s, D_MODEL)
    q_s, _, _, k_s, v_s, zr_s = _in_proj(
        xs, mod_rows(0, 0, False), mod_rows(0, 1, False), g1, w_bf, wt_bf, tm=256, transposed=False)
    q5 = q_s.reshape(bs, ts, 2 * H_A, DK_A)
    eye_hm = jnp.eye(2 * H_A, dtype=BF16)
    qbd = (q5.transpose(0, 2, 3, 1)[:, :, :, None, :] * eye_hm[None, :, None, :, None])
    qbd = qbd.reshape(bs, C_QA, 2 * H_A * ts)
    n_pool = cache_k.shape[1]
    oa_s = _attn_sample(page_table, qbd, k_s.reshape(bs, ts, C_QA), v_s.reshape(bs, ts, W_A),
                        cache_k[l].reshape(n_pool, PAGE_SIZE, C_QA),
                        cache_v[l].reshape(n_pool, PAGE_SIZE, W_A), g_row, seg, lams)

    first_s = jnp.zeros((bs, ts, C_R), F32).at[:, 0].set(state_shift[l]).reshape(n_s, C_R)
    prep_s = _rwkv_prep(zr_s, first_s, ts, 256, rw)
    or_s, s_s = _rwkv_scan(prep_s, lnw, lnb, avg, _block_diag_state(state_wkv[l]),
                           chunk=ts, carry=False)
    mods_s = (mod_rows(0, 2, False), mod_rows(1, 0, False), mod_rows(1, 1, False), mod_rows(1, 2, False))
    y_s = _ffn(xs, oa_s.reshape(n_s, W_A), or_s, mods_s, g2, gf, wo_bf, wr, br, wg_bf, wu_bf, wd_bf,
               tm=512, oa_transposed=False)

    return (
        y_p.reshape(bp, sp, D_MODEL),
        y_s.reshape(bs, ts, D_MODEL),
        k_p.reshape(1, bp, sp, H_A, 2, DK_A),
        v_p.reshape(1, bp, sp, H_A, DV_A),
        _diag_state(s_p[None]).astype(state_wkv.dtype)[None],
        zr_p[n_p - 1:].reshape(1, bp, C_R),
        k_s.reshape(1, bs, ts, H_A, 2, DK_A),
        v_s.reshape(1, bs, ts, H_A, DV_A),
        _diag_state(s_s).astype(state_wkv.dtype)[None],
        zr_s.reshape(bs, ts, C_R)[:, -1][None],
    )
```

```python
import functools
import math

import jax
import jax.numpy as jnp
from jax import lax
from jax.experimental import pallas as pl
from jax.experimental.pallas import tpu as pltpu

F32 = jnp.float32
BF16 = jnp.bfloat16

D_MODEL = 1024
PAGE_SIZE = 128
H_A = 8
DK_A = 32
DV_A = 64
H_R = 8
HD_R = 64
D_DECAY_LORA = 64
D_AAA_LORA = 64
D_GATE_LORA = 128
N_GROUPS = 4
EXPERTS_PER_GROUP = 4
N_EXPERTS = N_GROUPS * EXPERTS_PER_GROUP
D_EXPERT = 512
NORM_EPS = 1e-6
LN_X_EPS = 64e-5
NEG_INF = -1e30
W_A = H_A * DV_A
W_R = H_R * HD_R
C_QA = H_A * 2 * DK_A
C_A = 2 * C_QA + W_A
C_R = 3 * W_R + D_DECAY_LORA + D_AAA_LORA + D_GATE_LORA
LAM_INIT = 0.8 - 0.6 * math.exp(-0.3 * 0)

LANES = 128
LOG2E = 1.4426950408889634
Q_SCALE = DK_A ** -0.5 * LOG2E
VMEM_LIMIT = 56 * 1024 * 1024


def _dot(a, b):
    return jnp.dot(a, b, preferred_element_type=F32)


def _dot_nt(a, b):
    return lax.dot_general(a, b, (((1,), (1,)), ((), ())), preferred_element_type=F32)


def _dot_tn(a, b):
    return lax.dot_general(a, b, (((0,), (0,)), ((), ())), preferred_element_type=F32)


def _split3(x):
    hi = x.astype(BF16)
    r1 = x - hi.astype(F32)
    mid = r1.astype(BF16)
    lo = (r1 - mid.astype(F32)).astype(BF16)
    return hi, mid, lo


def _dot_exact_rhs(x, m_bf16):
    hi, mid, lo = _split3(x)
    return _dot(hi, m_bf16) + _dot(mid, m_bf16) + _dot(lo, m_bf16)


def _dot_exact_lhs(m_bf16, x):
    hi, mid, lo = _split3(x)
    return _dot(m_bf16, hi) + _dot(m_bf16, mid) + _dot(m_bf16, lo)


def _dot_f32(a, b):
    ah = a.astype(BF16)
    al = (a - ah.astype(F32)).astype(BF16)
    bh = b.astype(BF16)
    bl = (b - bh.astype(F32)).astype(BF16)
    return _dot(ah, bh) + _dot(ah, bl) + _dot(al, bh)


def _sigmoid(x):
    return 1.0 / (1.0 + jnp.exp(-x))


def _params(*sem):
    return pltpu.CompilerParams(dimension_semantics=sem, vmem_limit_bytes=VMEM_LIMIT)


def _ada_kernel(c_ref, w_ref, b_ref, o_ref):
    c = c_ref[...]
    s = (c * _sigmoid(c)).astype(BF16)
    o_ref[0] = _dot(s, w_ref[0]) + b_ref[0]


def _ada_mod(c_all, ada_w, ada_b):
    r = c_all.shape[0]
    n_col = 3
    return pl.pallas_call(
        _ada_kernel,
        grid=(2, n_col),
        in_specs=[
            pl.BlockSpec((r, D_MODEL), lambda s, j: (0, 0)),
            pl.BlockSpec((1, D_MODEL, D_MODEL), lambda s, j: (s, 0, j)),
            pl.BlockSpec((1, 1, D_MODEL), lambda s, j: (s, 0, j)),
        ],
        out_specs=pl.BlockSpec((1, r, D_MODEL), lambda s, j: (s, 0, j)),
        out_shape=jax.ShapeDtypeStruct((2, r, 3 * D_MODEL), F32),
        compiler_params=_params("arbitrary", "arbitrary"),
        name="ada_mod",
    )(c_all, ada_w, ada_b.reshape(2, 1, 3 * D_MODEL))


def _modulated_norm(x, g, sc, sh):
    ms = jnp.mean(x * x, axis=-1, keepdims=True)
    return (x * lax.rsqrt(ms + NORM_EPS) * g) * (1.0 + sc) + sh


def _inproj_kernel(x_ref, sh_ref, sc_ref, g_ref, w_ref, wt_ref,
                   q_ref, kb_ref, vb_ref, k_ref, v_ref, zr_ref, *, transposed):
    hb = _modulated_norm(x_ref[...], g_ref[...], sc_ref[...], sh_ref[...]).astype(BF16)
    zk = _dot(hb, w_ref[:, C_QA:2 * C_QA])
    k_ref[...] = zk
    kb_ref[...] = zk.astype(BF16)
    zv = _dot(hb, w_ref[:, 2 * C_QA:C_A])
    v_ref[...] = zv
    zr_ref[...] = _dot(hb, w_ref[:, C_A:])
    if transposed:
        q_ref[...] = (_dot_nt(wt_ref[0:C_QA], hb) * Q_SCALE).astype(BF16)
        vb_ref[...] = _dot_nt(wt_ref[C_QA:C_QA + W_A], hb).astype(BF16)
    else:
        q_ref[...] = (_dot(hb, w_ref[:, 0:C_QA]) * Q_SCALE).astype(BF16)
        vb_ref[...] = zv.astype(BF16)


def _in_proj(x, sh, sc, g, w_bf, wt_bf, *, tm, transposed):
    n = x.shape[0]
    per_tok = sh.shape[0] != 1
    mod_spec = (pl.BlockSpec((tm, D_MODEL), lambda i: (i, 0)) if per_tok
                else pl.BlockSpec((1, D_MODEL), lambda i: (0, 0)))
    tok = lambda w: pl.BlockSpec((tm, w), lambda i: (i, 0))
    if transposed:
        qv_spec = pl.BlockSpec((C_QA, tm), lambda i: (0, i))
        qv_shape = jax.ShapeDtypeStruct((C_QA, n), BF16)
    else:
        qv_spec = tok(C_QA)
        qv_shape = jax.ShapeDtypeStruct((n, C_QA), BF16)
    return pl.pallas_call(
        functools.partial(_inproj_kernel, transposed=transposed),
        grid=(n // tm,),
        in_specs=[
            tok(D_MODEL), mod_spec, mod_spec,
            pl.BlockSpec((1, D_MODEL), lambda i: (0, 0)),
            pl.BlockSpec(w_bf.shape, lambda i: (0, 0)),
            pl.BlockSpec(wt_bf.shape, lambda i: (0, 0)),
        ],
        out_specs=[qv_spec, tok(C_QA), qv_spec, tok(C_QA), tok(W_A), tok(C_R)],
        out_shape=[
            qv_shape,
            jax.ShapeDtypeStruct((n, C_QA), BF16),
            qv_shape,
            jax.ShapeDtypeStruct((n, C_QA), F32),
            jax.ShapeDtypeStruct((n, W_A), F32),
            jax.ShapeDtypeStruct((n, C_R), F32),
        ],
        compiler_params=_params("arbitrary"),
        name="in_proj_t" if transposed else "in_proj",
    )(x, sh, sc, g, w_bf, wt_bf)


def _lambda(q1_ref, k1_ref, q2_ref, k2_ref):
    l1 = jnp.sum(q1_ref[...] * k1_ref[...], axis=-1, keepdims=True)
    l2 = jnp.sum(q2_ref[...] * k2_ref[...], axis=-1, keepdims=True)
    return jnp.exp(l1) - jnp.exp(l2) + LAM_INIT


def _attn_prompt_kernel(qt_ref, k_ref, vt_ref, g_ref, lq1, lk1, lq2, lk2, o_ref,
                        m_scr, l_scr, acc_scr, *, tq, tk):
    i = pl.program_id(1)
    n_sub = tq // tk
    m_scr[...] = jnp.full(m_scr.shape, NEG_INF, F32)
    l_scr[...] = jnp.zeros(l_scr.shape, F32)
    acc_scr[...] = jnp.zeros(acc_scr.shape, F32)

    qt = qt_ref[...]
    row_q = lax.broadcasted_iota(jnp.int32, (LANES, tq), 0)
    qtm = [jnp.where((row_q >= DK_A * i4) & (row_q < DK_A * (i4 + 1)), qt, jnp.zeros_like(qt))
           for i4 in range(4)]
    row_v = lax.broadcasted_iota(jnp.int32, (LANES, tk), 0)
    first_head = row_q < DV_A

    def step(j, diag_offset):
        start = pl.multiple_of(j * tk, tk)
        kblk = k_ref[pl.ds(start, tk), :]
        vtblk = vt_ref[:, pl.ds(start, tk)]
        if diag_offset is not None:
            visible = (lax.broadcasted_iota(jnp.int32, (tk, tq), 0) + diag_offset
                       <= lax.broadcasted_iota(jnp.int32, (tk, tq), 1))
        pts, alphas = [], []
        for i4 in range(4):
            st = _dot(kblk, qtm[i4])
            if diag_offset is not None:
                st = jnp.where(visible, st, NEG_INF)
            m_prev = m_scr[i4]
            m_new = jnp.maximum(m_prev, jnp.max(st, axis=0, keepdims=True))
            alpha = jnp.exp2(m_prev - m_new)
            pt = jnp.exp2(st - m_new)
            l_scr[i4] = alpha * l_scr[i4] + jnp.sum(pt, axis=0, keepdims=True)
            m_scr[i4] = m_new
            pts.append(pt.astype(BF16))
            alphas.append(alpha)
        vbd = jnp.concatenate(
            [jnp.where(row_v < DV_A, vtblk, jnp.zeros_like(vtblk)),
             jnp.where(row_v >= DV_A, vtblk, jnp.zeros_like(vtblk))], axis=1)
        for m in range(2):
            pcat = jnp.concatenate([pts[m], pts[2 + m]], axis=0)
            alpha_m = jnp.where(first_head, alphas[m], alphas[2 + m])
            acc_scr[m] = acc_scr[m] * alpha_m + _dot(vbd, pcat)

    def body(jj, carry):
        for u in range(n_sub):
            step(jj * n_sub + u, None)
        return carry

    lax.fori_loop(0, i, body, 0)
    for u in range(n_sub):
        step(i * n_sub + u, u * tk)

    lam = _lambda(lq1, lk1, lq2, lk2)
    o0 = acc_scr[0] / jnp.where(first_head, l_scr[0], l_scr[2])
    o1 = acc_scr[1] / jnp.where(first_head, l_scr[1], l_scr[3])
    o = (o0 - lam * o1).reshape(2, DV_A, tq)
    ms = jnp.mean(o * o, axis=1, keepdims=True)
    o = o * lax.rsqrt(ms + NORM_EPS)
    o = o.reshape(LANES, tq) * g_ref[:, 0:1] * (1.0 - LAM_INIT)
    o_ref[...] = o.astype(o_ref.dtype)


def _attn_prompt(qt, k_bf, vt, g_col, lams, *, tq, tk):
    s = qt.shape[1]
    lam_spec = pl.BlockSpec((1, DK_A), lambda p, i: (0, 0))
    return pl.pallas_call(
        functools.partial(_attn_prompt_kernel, tq=tq, tk=tk),
        grid=(H_A // 2, s // tq),
        in_specs=[
            pl.BlockSpec((LANES, tq), lambda p, i: (p, i)),
            pl.BlockSpec((s, LANES), lambda p, i: (0, p)),
            pl.BlockSpec((LANES, s), lambda p, i: (p, 0)),
            pl.BlockSpec((LANES, LANES), lambda p, i: (0, 0)),
            lam_spec, lam_spec, lam_spec, lam_spec,
        ],
        out_specs=pl.BlockSpec((LANES, tq), lambda p, i: (p, i)),
        out_shape=jax.ShapeDtypeStruct((W_A, s), BF16),
        scratch_shapes=[
            pltpu.VMEM((4, 1, tq), F32),
            pltpu.VMEM((4, 1, tq), F32),
            pltpu.VMEM((2, LANES, tq), F32),
        ],
        compiler_params=_params("arbitrary", "arbitrary"),
        name="attn_prompt",
    )(qt, k_bf, vt, g_col, *lams)


PAGES_PER_STEP = 8
SEQS_PER_STEP = 2


def _attn_sample_kernel(pt_ref, qbd_ref, kn_ref, vn_ref, g_ref, seg_ref, lq1, lk1, lq2, lk2, *rest,
                        n_steps, t_new):
    n_pg = SEQS_PER_STEP * PAGES_PER_STEP
    k_refs, v_refs = rest[:n_pg], rest[n_pg:2 * n_pg]
    o_ref, m_scr, l_scr, acc_scr = rest[2 * n_pg:]
    c = pl.program_id(1)

    @pl.when(c == 0)
    def _():
        m_scr[...] = jnp.full(m_scr.shape, NEG_INF, F32)
        l_scr[...] = jnp.zeros(l_scr.shape, F32)
        acc_scr[...] = jnp.zeros(acc_scr.shape, F32)

    def update(q, s, pv):
        m_prev = m_scr[q]
        m_new = jnp.maximum(m_prev, jnp.max(s, axis=1, keepdims=True))
        alpha = jnp.exp2(m_prev - m_new)
        p = jnp.exp2(s - m_new)
        l_scr[q] = alpha * l_scr[q] + jnp.sum(p, axis=1, keepdims=True)
        m_scr[q] = m_new
        acc_scr[q] = acc_scr[q] * alpha + pv(p.astype(BF16))

    for q in range(SEQS_PER_STEP):
        pages = slice(q * PAGES_PER_STEP, (q + 1) * PAGES_PER_STEP)
        kt = jnp.concatenate([r[0] for r in k_refs[pages]], axis=1).astype(BF16)
        vt = jnp.concatenate([r[0] for r in v_refs[pages]], axis=1).astype(BF16)
        update(q, _dot(qbd_ref[q], kt), lambda p, vt=vt: _dot_nt(p, vt))

    @pl.when(c == n_steps - 1)
    def _():
        lam = _lambda(lq1, lk1, lq2, lk2)
        pad = jnp.zeros((16 - t_new, C_QA), F32)
        trow = lax.broadcasted_iota(jnp.int32, (LANES, 16), 0) % t_new
        jcol = lax.broadcasted_iota(jnp.int32, (LANES, 16), 1)
        lane_head = lax.broadcasted_iota(jnp.int32, (t_new, W_A), 1) // DV_A
        for q in range(SEQS_PER_STEP):
            kn = jnp.concatenate([kn_ref[q], pad], axis=0).astype(BF16)
            vn = jnp.concatenate([vn_ref[q], pad], axis=0).astype(BF16)
            s_new = jnp.where(jcol <= trow, _dot_nt(qbd_ref[q], kn), NEG_INF)
            update(q, s_new, lambda p, vn=vn: _dot(p, vn))

            accn = acc_scr[q] / l_scr[q]
            outs = []
            for m in range(2):
                o_m = jnp.zeros((t_new, W_A), F32)
                for h in range(H_A):
                    r0 = h * 2 * t_new + m * t_new
                    o_m = jnp.where(lane_head == h, accn[r0:r0 + t_new, :], o_m)
                outs.append(o_m)
            o = outs[0] - lam * outs[1]
            ms = _dot_exact_rhs(o * o, seg_ref[...]) * (1.0 / DV_A)
            o_ref[q] = o * lax.rsqrt(ms + NORM_EPS) * g_ref[...] * (1.0 - LAM_INIT)


def _attn_sample(page_table, qbd, k_new, v_new, kt_pages, vt_pages, g_row, seg, lams):
    b, t_new = k_new.shape[0], k_new.shape[1]
    n_pages = page_table.shape[1]
    n_steps = n_pages // PAGES_PER_STEP
    nq = SEQS_PER_STEP
    pt_flat = page_table.reshape(-1)

    def page_spec(q, j):
        return pl.BlockSpec(
            (1, C_QA, PAGE_SIZE),
            lambda bi, c, pt: (pt[(bi * nq + q) * n_pages + c * PAGES_PER_STEP + j], 0, 0))

    page_specs = [page_spec(q, j) for q in range(nq) for j in range(PAGES_PER_STEP)]
    lam_spec = pl.BlockSpec((1, DK_A), lambda bi, c, pt: (0, 0))
    seq = lambda shp: pl.BlockSpec(shp, lambda bi, c, pt: (bi, 0, 0))
    grid_spec = pltpu.PrefetchScalarGridSpec(
        num_scalar_prefetch=1,
        grid=(b // nq, n_steps),
        in_specs=[
            seq((nq, LANES, C_QA)), seq((nq, t_new, C_QA)), seq((nq, t_new, W_A)),
            pl.BlockSpec((1, W_A), lambda bi, c, pt: (0, 0)),
            pl.BlockSpec((W_A, W_A), lambda bi, c, pt: (0, 0)),
            lam_spec, lam_spec, lam_spec, lam_spec,
        ] + page_specs * 2,
        out_specs=seq((nq, t_new, W_A)),
        scratch_shapes=[
            pltpu.VMEM((nq, LANES, 1), F32),
            pltpu.VMEM((nq, LANES, 1), F32),
            pltpu.VMEM((nq, LANES, W_A), F32),
        ],
    )
    return pl.pallas_call(
        functools.partial(_attn_sample_kernel, n_steps=n_steps, t_new=t_new),
        grid_spec=grid_spec,
        out_shape=jax.ShapeDtypeStruct((b, t_new, W_A), F32),
        compiler_params=_params("arbitrary", "arbitrary"),
        name="attn_sample",
    )(pt_flat, qbd, k_new, v_new, g_row, seg, *lams,
      *([kt_pages] * (nq * PAGES_PER_STEP)), *([vt_pages] * (nq * PAGES_PER_STEP)))


def _rwkv_prep_kernel(zr_ref, first_ref, mu_ref, w0_ref, w2_ref, a0_ref, a2_ref, g2_ref,
                      kk_ref, ka_ref, rk_ref, seg_ref,
                      r_o, lw_o, k_o, v_o, kn_o, b_o, g_o, bonus_o, *, period):
    z = zr_ref[...]
    row = lax.broadcasted_iota(jnp.int32, z.shape, 0)
    first = first_ref[...] if first_ref.shape[0] == z.shape[0] else first_ref[0:1, :]
    z_prev = jnp.where((row & (period - 1)) == 0, first, pltpu.roll(z, 1, 0))
    z = z + (z_prev - z) * mu_ref[...]
    o1, o2, o3 = W_R, 2 * W_R, 3 * W_R
    o5 = o3 + D_DECAY_LORA + D_AAA_LORA
    r, k, v = z[:, :o1], z[:, o1:o2], z[:, o2:o3]
    zwa = z[:, o3:o5]
    zg = z[:, o5:]
    w_pre = w0_ref[...] + _dot_f32(jnp.tanh(zwa), w2_ref[...])
    w_raw = -(jnp.maximum(-w_pre, 0.0) + jnp.log(1.0 + jnp.exp(-jnp.abs(w_pre)))) - 0.5
    lw_o[...] = -jnp.exp(w_raw)
    a = _sigmoid(a0_ref[...] + _dot(zwa.astype(BF16), a2_ref[...]))
    g_o[...] = _dot(_sigmoid(zg).astype(BF16), g2_ref[...])
    kk = k * kk_ref[...]
    ssq = _dot_exact_rhs(kk * kk, seg_ref[...])
    kn = kk * lax.rsqrt(ssq + 1e-12)
    k2 = k * (1.0 + (a - 1.0) * ka_ref[...])
    rk = _dot_exact_rhs(r * k2 * rk_ref[...], seg_ref[...])
    r_o[...] = r
    k_o[...] = k2
    v_o[...] = v
    kn_o[...] = kn
    b_o[...] = kn * a
    bonus_o[...] = rk * v


def _rwkv_prep(zr, first, period, tm, p):
    n = zr.shape[0]
    assert period & (period - 1) == 0
    tok = lambda w: pl.BlockSpec((tm, w), lambda i: (i, 0))
    full = lambda a: pl.BlockSpec(a.shape, lambda i: (0,) * a.ndim)
    first_spec = tok(C_R) if first.shape[0] == n else pl.BlockSpec((8, C_R), lambda i: (i, 0))
    consts = [p["mu"], p["w0"], p["w2"], p["a0"], p["a2"], p["g2"], p["kk"], p["ka"], p["rk"], p["seg"]]
    return pl.pallas_call(
        functools.partial(_rwkv_prep_kernel, period=period),
        grid=(n // tm,),
        in_specs=[tok(C_R), first_spec] + [full(a) for a in consts],
        out_specs=[tok(W_R)] * 8,
        out_shape=[jax.ShapeDtypeStruct((n, W_R), F32)] * 8,
        compiler_params=_params("arbitrary"),
        name="rwkv_prep",
    )(zr, first, *consts)


SUPER = 256


def _rwkv_scan_kernel(r_ref, lw_ref, k_ref, v_ref, kn_ref, b_ref, g_ref, bonus_ref,
                      lnw_ref, lnb_ref, avg_ref, s_in_ref, o_ref, s_out_ref, *, chunk, carry):
    n_blk = SUPER // chunk
    n_pair = H_R // 2
    shift = chunk.bit_length() - 1
    ri = lax.broadcasted_iota(jnp.int32, (SUPER, SUPER), 0)
    ci = lax.broadcasted_iota(jnp.int32, (SUPER, SUPER), 1)
    same = (ri >> shift) == (ci >> shift)
    strict = same & (ci < ri)
    incl = same & (ci <= ri)
    m_incl = jnp.where(incl, 1.0, 0.0).astype(BF16)

    if carry:
        @pl.when(pl.program_id(0) == 0)
        def _():
            s_out_ref[...] = s_in_ref[...]

    lw = lw_ref[...]
    cum = _dot_exact_lhs(m_incl, lw)
    gi = jnp.exp(cum)
    a_t = -kn_ref[...] * jnp.exp(cum - lw)
    ginv = jnp.exp(-cum)
    b_t = b_ref[...] * ginv
    k_t = k_ref[...] * ginv
    r_t = r_ref[...] * gi
    v_all = v_ref[...]

    lane = lax.broadcasted_iota(jnp.int32, (SUPER, LANES), 1)
    bd = ((lax.broadcasted_iota(jnp.int32, (LANES, LANES), 0) // HD_R)
          == (lax.broadcasted_iota(jnp.int32, (LANES, LANES), 1) // HD_R))
    col_blk = lax.broadcasted_iota(jnp.int32, (LANES, SUPER), 1) >> shift

    pre = []
    for p in range(n_pair):
        ls = slice(p * LANES, (p + 1) * LANES)
        a_p, b_p, k_p, r_p, v_p = a_t[:, ls], b_t[:, ls], k_t[:, ls], r_t[:, ls], v_all[:, ls]
        b_pb, k_pb = b_p.astype(BF16), k_p.astype(BF16)
        a2_p = jnp.zeros((SUPER, LANES), F32)
        p0_p = jnp.zeros((SUPER, LANES), F32)
        y0_p = jnp.zeros((SUPER, LANES), F32)
        wrbs = []
        for hh in range(2):
            hm = (lane // HD_R) == hh
            a_m = jnp.where(hm, a_p, 0.0)
            a_mb = a_m.astype(BF16)
            r_mb = jnp.where(hm, r_p, 0.0).astype(BF16)
            v_mb = jnp.where(hm, v_p, 0.0).astype(BF16)
            l_ab = jnp.where(strict, _dot_nt(a_mb, b_pb), 0.0)
            l_ak = jnp.where(strict, _dot_nt(a_mb, k_pb), 0.0)
            w_rb = jnp.where(incl, _dot_nt(r_mb, b_pb), 0.0)
            w_rk = jnp.where(incl, _dot_nt(r_mb, k_pb), 0.0)
            t_m = l_ab
            l_pow = l_ab
            for _ in range(shift - 1):
                lb = l_pow.astype(BF16)
                l_pow = _dot(lb, lb)
                t_m = t_m + l_pow + _dot(t_m.astype(BF16), l_pow.astype(BF16))
            t_mb = t_m.astype(BF16)
            a2_p = a2_p + a_m + _dot(t_mb, a_mb)
            u0 = _dot(l_ak.astype(BF16), v_mb)
            p0_p = p0_p + u0 + _dot(t_mb, u0.astype(BF16))
            y0_p = y0_p + _dot(w_rk.astype(BF16), v_mb)
            wrbs.append(w_rb.astype(BF16))
        pre.append(dict(
            ar=jnp.concatenate([a2_p, r_p], axis=0).astype(BF16),
            bk=jnp.concatenate([b_pb, k_pb], axis=0),
            p0t=jnp.transpose(p0_p),
            vt=jnp.transpose(v_p).astype(BF16),
            y0=y0_p, wrbs=wrbs))

    states = [s_out_ref[p] for p in range(n_pair)] if carry else [None] * n_pair
    pt_acc = [jnp.zeros((LANES, SUPER), F32) for _ in range(n_pair)]
    yrt_acc = [jnp.zeros((LANES, SUPER), F32) for _ in range(n_pair)]
    for c in range(n_blk):
        in_blk = col_blk == c
        for p in range(n_pair):
            q = pre[p]
            s = states[p] if carry else s_in_ref[c, p]
            out = _dot_nt(s.astype(BF16), q["ar"])
            ptm = jnp.where(in_blk, out[:, :SUPER] + q["p0t"], 0.0)
            pt_acc[p] = pt_acc[p] + ptm
            yrt_acc[p] = jnp.where(in_blk, out[:, SUPER:], yrt_acc[p])
            lhs = jnp.concatenate([ptm.astype(BF16), jnp.where(in_blk, q["vt"], jnp.zeros_like(q["vt"]))], axis=1)
            upd = _dot(lhs, q["bk"])
            g_end = gi[(c + 1) * chunk - 1:(c + 1) * chunk, p * LANES:(p + 1) * LANES]
            s = (s + jnp.where(bd, upd, 0.0)) * g_end
            if carry:
                states[p] = s
            else:
                s_out_ref[c, p] = s

    for p in range(n_pair):
        ls = slice(p * LANES, (p + 1) * LANES)
        q = pre[p]
        if carry:
            s_out_ref[p] = states[p]
        p_full = jnp.transpose(pt_acc[p])
        y = jnp.transpose(yrt_acc[p]) + q["y0"]
        for hh in range(2):
            hm = (lane // HD_R) == hh
            y = y + _dot(q["wrbs"][hh], jnp.where(hm, p_full, 0.0).astype(BF16))
        mean = _dot_exact_rhs(y, avg_ref[...])
        yc = y - mean
        var = _dot_exact_rhs(yc * yc, avg_ref[...])
        yn = yc * lax.rsqrt(var + LN_X_EPS) * lnw_ref[:, ls] + lnb_ref[:, ls]
        o_ref[:, ls] = ((yn + bonus_ref[:, ls]) * g_ref[:, ls]).astype(o_ref.dtype)


def _rwkv_scan(prep, lnw, lnb, avg, s_in, *, chunk, carry):
    n = prep[0].shape[0]
    n_blk = SUPER // chunk
    tok = pl.BlockSpec((SUPER, W_R), lambda i: (i, 0))
    row = pl.BlockSpec((1, W_R), lambda i: (0, 0))
    if carry:
        s_spec = pl.BlockSpec(s_in.shape, lambda i: (0, 0, 0))
    else:
        s_spec = pl.BlockSpec((n_blk,) + s_in.shape[1:], lambda i: (i, 0, 0, 0))
    return pl.pallas_call(
        functools.partial(_rwkv_scan_kernel, chunk=chunk, carry=carry),
        grid=(n // SUPER,),
        in_specs=[tok] * 8 + [row, row, pl.BlockSpec((LANES, LANES), lambda i: (0, 0)), s_spec],
        out_specs=[tok, s_spec],
        out_shape=[jax.ShapeDtypeStruct((n, W_R), BF16), jax.ShapeDtypeStruct(s_in.shape, F32)],
        compiler_params=_params("arbitrary"),
        name="rwkv_scan_carry" if carry else "rwkv_scan_seq",
    )(*prep, lnw, lnb, avg, s_in)


ROUTE_W = 128
EXPERT_LANE0 = 8


def _route(logits):
    lane = lax.broadcasted_iota(jnp.int32, logits.shape, 1)
    big = jnp.int32(1 << 20)
    g_log = jnp.where(lane < N_GROUPS, logits, -jnp.inf)
    g_max = jnp.max(g_log, axis=-1, keepdims=True)
    g_idx = jnp.min(jnp.where(g_log == g_max, lane, big), axis=-1, keepdims=True)
    g_w = 1.0 / jnp.sum(jnp.exp(g_log - g_max), axis=-1, keepdims=True)
    e_lo = EXPERT_LANE0 + g_idx * EXPERTS_PER_GROUP
    e_log = jnp.where((lane >= e_lo) & (lane < e_lo + EXPERTS_PER_GROUP), logits, -jnp.inf)
    e_max = jnp.max(e_log, axis=-1, keepdims=True)
    i1 = jnp.min(jnp.where(e_log == e_max, lane, big), axis=-1, keepdims=True)
    e_log2 = jnp.where(lane == i1, -jnp.inf, e_log)
    e_max2 = jnp.max(e_log2, axis=-1, keepdims=True)
    i2 = jnp.min(jnp.where(e_log2 == e_max2, lane, big), axis=-1, keepdims=True)
    p2 = jnp.exp(e_max2 - e_max)
    w1 = g_w / (1.0 + p2)
    w2 = g_w * p2 / (1.0 + p2)
    return jnp.where(lane == i1, w1, 0.0) + jnp.where(lane == i2, w2, 0.0)


def _ffn_kernel(x_ref, oa_ref, or_ref, gt1_ref, sh_ref, sc_ref, gt2_ref, g2_ref, gf_ref,
                wo_ref, wr_ref, br_ref, wg_ref, wu_ref, wd_ref, y_ref,
                x1_scr, h_scr, comb_scr, acc_scr, *, oa_transposed):
    e = pl.program_id(1)

    @pl.when(e == 0)
    def _():
        if oa_transposed:
            mix = _dot_tn(oa_ref[...], wo_ref[0:W_A])
        else:
            mix = _dot(oa_ref[...].astype(BF16), wo_ref[0:W_A])
        mix = mix + _dot(or_ref[...], wo_ref[W_A:])
        x1 = x_ref[...] + gt1_ref[...] * mix
        x1_scr[...] = x1
        h = _modulated_norm(x1, g2_ref[...], sc_ref[...], sh_ref[...])
        h_scr[...] = h.astype(BF16)
        comb_scr[...] = _route(_dot_f32(h, wr_ref[...]) + br_ref[...])
        acc_scr[...] = jnp.zeros(acc_scr.shape, F32)

    hb = h_scr[...]
    hg = _dot(hb, wg_ref[0])
    hu = _dot(hb, wu_ref[0])
    lane = lax.broadcasted_iota(jnp.int32, comb_scr.shape, 1)
    comb_e = jnp.sum(jnp.where(lane == EXPERT_LANE0 + e, comb_scr[...], 0.0), axis=-1, keepdims=True)
    act = (hg * _sigmoid(hg)) * hu * comb_e
    acc_scr[...] += _dot(act.astype(BF16), wd_ref[0])

    @pl.when(e == N_EXPERTS - 1)
    def _():
        x2 = x1_scr[...] + gt2_ref[...] * acc_scr[...]
        ms = jnp.mean(x2 * x2, axis=-1, keepdims=True)
        y_ref[...] = x2 * lax.rsqrt(ms + NORM_EPS) * gf_ref[...]


def _ffn(x, oa, o_r, mods, g2, gf, wo_bf, wr, br, wg_bf, wu_bf, wd_bf, *, tm, oa_transposed):
    n = x.shape[0]
    per_tok = mods[0].shape[0] != 1
    mod_spec = (pl.BlockSpec((tm, D_MODEL), lambda i, e: (i, 0)) if per_tok
                else pl.BlockSpec((1, D_MODEL), lambda i, e: (0, 0)))
    tok = lambda w: pl.BlockSpec((tm, w), lambda i, e: (i, 0))
    oa_spec = pl.BlockSpec((W_A, tm), lambda i, e: (0, i)) if oa_transposed else tok(W_A)
    row = pl.BlockSpec((1, D_MODEL), lambda i, e: (0, 0))
    return pl.pallas_call(
        functools.partial(_ffn_kernel, oa_transposed=oa_transposed),
        grid=(n // tm, N_EXPERTS),
        in_specs=[
            tok(D_MODEL), oa_spec, tok(W_R), mod_spec, mod_spec, mod_spec, mod_spec, row, row,
            pl.BlockSpec(wo_bf.shape, lambda i, e: (0, 0)),
            pl.BlockSpec(wr.shape, lambda i, e: (0, 0)),
            pl.BlockSpec((1, ROUTE_W), lambda i, e: (0, 0)),
            pl.BlockSpec((1, D_MODEL, D_EXPERT), lambda i, e: (e, 0, 0)),
            pl.BlockSpec((1, D_MODEL, D_EXPERT), lambda i, e: (e, 0, 0)),
            pl.BlockSpec((1, D_EXPERT, D_MODEL), lambda i, e: (e, 0, 0)),
        ],
        out_specs=tok(D_MODEL),
        out_shape=jax.ShapeDtypeStruct((n, D_MODEL), F32),
        scratch_shapes=[
            pltpu.VMEM((tm, D_MODEL), F32),
            pltpu.VMEM((tm, D_MODEL), BF16),
            pltpu.VMEM((tm, ROUTE_W), F32),
            pltpu.VMEM((tm, D_MODEL), F32),
        ],
        compiler_params=_params("arbitrary", "arbitrary"),
        name="ffn_t" if oa_transposed else "ffn",
    )(x, oa, o_r, *mods, g2, gf, wo_bf, wr, br, wg_bf, wu_bf, wd_bf)


def _block_diag_state(s):
    b = s.shape[0]
    s = s.reshape(b, H_R // 2, 2, HD_R, HD_R)
    z = jnp.zeros_like(s[:, :, 0])
    top = jnp.concatenate([s[:, :, 0], z], axis=-1)
    bot = jnp.concatenate([z, s[:, :, 1]], axis=-1)
    return jnp.concatenate([top, bot], axis=-2)


def _diag_state(sbd):
    b = sbd.shape[0]
    h0 = sbd[:, :, :HD_R, :HD_R]
    h1 = sbd[:, :, HD_R:, HD_R:]
    return jnp.stack([h0, h1], axis=2).reshape(b, H_R, HD_R, HD_R)


def kernel(x_prompt, x_sample, c_prompt, c_sample, cache_k, cache_v, state_wkv, state_shift, page_table, ada_w, ada_b, norm_mix_g, norm_ffn_g, w_in, w_out, lam_q1, lam_k1, lam_q2, lam_k2, subln_g, rw_mu, rw_w0, rw_w2, rw_a0, rw_a2, rw_g2, rw_kk, rw_ka, rw_rk, rw_lnw, rw_lnb, moe_wg, moe_bg, moe_we, moe_be, moe_w_gate, moe_w_up, moe_w_down, final_g):
    bp, sp, _ = x_prompt.shape
    bs, ts, _ = x_sample.shape
    assert bp == 1 and ada_w.shape[0] == 1, "single prompt row and single layer only"
    n_p, n_s = bp * sp, bs * ts
    l = 0

    w_bf = w_in[l].astype(BF16)
    wt_bf = jnp.concatenate([w_in[l][:, :C_QA].T, w_in[l][:, 2 * C_QA:C_A].T], axis=0).astype(BF16)
    wo_bf = w_out[l].astype(BF16)
    lams = [a[l].reshape(1, DK_A) for a in (lam_q1, lam_k1, lam_q2, lam_k2)]
    g_col = jnp.broadcast_to(jnp.tile(subln_g[l], 2)[:, None], (LANES, LANES))
    g_row = jnp.tile(subln_g[l], H_A).reshape(1, W_A)
    head_of = jnp.arange(W_R) // HD_R
    seg = (head_of[:, None] == head_of[None, :]).astype(BF16)
    avg = (seg[:LANES, :LANES].astype(F32) / HD_R).astype(BF16)
    zpad = jnp.zeros((D_DECAY_LORA, W_R), F32)
    rw = {
        "mu": rw_mu[l].reshape(1, C_R), "w0": rw_w0[l].reshape(1, W_R),
        "w2": jnp.concatenate([rw_w2[l], zpad], axis=0),
        "a0": rw_a0[l].reshape(1, W_R),
        "a2": jnp.concatenate([zpad, rw_a2[l]], axis=0).astype(BF16),
        "g2": rw_g2[l].astype(BF16),
        "kk": rw_kk[l].reshape(1, W_R), "ka": rw_ka[l].reshape(1, W_R),
        "rk": rw_rk[l].reshape(1, W_R), "seg": seg,
    }
    lnw, lnb = rw_lnw[l].reshape(1, W_R), rw_lnb[l].reshape(1, W_R)
    wr = jnp.zeros((D_MODEL, ROUTE_W), F32)
    wr = wr.at[:, :N_GROUPS].set(moe_wg[l]).at[:, EXPERT_LANE0:EXPERT_LANE0 + N_EXPERTS].set(moe_we[l])
    br = jnp.zeros((1, ROUTE_W), F32)
    br = br.at[0, :N_GROUPS].set(moe_bg[l]).at[0, EXPERT_LANE0:EXPERT_LANE0 + N_EXPERTS].set(moe_be[l])
    wg_bf, wu_bf, wd_bf = (a[l].astype(BF16) for a in (moe_w_gate, moe_w_up, moe_w_down))
    g1 = norm_mix_g[l].reshape(1, D_MODEL)
    g2 = norm_ffn_g[l].reshape(1, D_MODEL)
    gf = final_g.reshape(1, D_MODEL)

    r_pad = -(bs + bp) % 8
    c_all = jnp.concatenate([c_sample, c_prompt, jnp.zeros((r_pad, D_MODEL), F32)], axis=0)
    mods = _ada_mod(c_all, ada_w[l].astype(BF16), ada_b[l])

    def mod_rows(sub, part, prompt):
        m = mods[sub, :, part * D_MODEL:(part + 1) * D_MODEL]
        if prompt:
            return m[bs:bs + 1]
        return jnp.repeat(m[:bs], ts, axis=0)

    xp = x_prompt.reshape(n_p, D_MODEL)
    qt, kb, vt, k_p, v_p, zr_p = _in_proj(
        xp, mod_rows(0, 0, True), mod_rows(0, 1, True), g1, w_bf, wt_bf, tm=512, transposed=True)
    oa_t = _attn_prompt(qt, kb, vt, g_col, lams, tq=1024, tk=512)

    tm_r = 256
    halo = jnp.concatenate([jnp.zeros((1, C_R), F32), zr_p[tm_r - 1:n_p - 1:tm_r]], axis=0)
    first_p = jnp.zeros((n_p // tm_r, 8, C_R), F32).at[:, 0].set(halo).reshape(n_p // tm_r * 8, C_R)
    prep_p = _rwkv_prep(zr_p, first_p, tm_r, tm_r, rw)
    or_p, s_p = _rwkv_scan(prep_p, lnw, lnb, avg, jnp.zeros((H_R // 2, LANES, LANES), F32),
                           chunk=64, carry=True)
    mods_p = (mod_rows(0, 2, True), mod_rows(1, 0, True), mod_rows(1, 1, True), mod_rows(1, 2, True))
    y_p = _ffn(xp, oa_t, or_p, mods_p, g2, gf, wo_bf, wr, br, wg_bf, wu_bf, wd_bf,
               tm=512, oa_transposed=True)

    xs = x_sample.reshape(n_s, D_MODEL)
    q_s, _, _, k_s, v_s, zr_s = _in_proj(
        xs, mod_rows(0, 0, False), mod_rows(0, 1, False), g1, w_bf, wt_bf, tm=256, transposed=False)
    q5 = q_s.reshape(bs, ts, 2 * H_A, DK_A)
    eye_hm = jnp.eye(2 * H_A, dtype=BF16)
    qbd = q5.transpose(0, 2, 1, 3)[:, :, :, None, :] * eye_hm[None, :, None, :, None]
    qbd = qbd.reshape(bs, 2 * H_A * ts, C_QA)
    n_pool = cache_k.shape[1]
    kt_pages = cache_k[l].transpose(0, 2, 3, 4, 1).reshape(n_pool, C_QA, PAGE_SIZE)
    vt_pages = cache_v[l].transpose(0, 2, 3, 1).reshape(n_pool, W_A, PAGE_SIZE)
    oa_s = _attn_sample(page_table, qbd, k_s.reshape(bs, ts, C_QA), v_s.reshape(bs, ts, W_A),
                        kt_pages, vt_pages, g_row, seg, lams)

    first_s = jnp.zeros((bs, ts, C_R), F32).at[:, 0].set(state_shift[l]).reshape(n_s, C_R)
    prep_s = _rwkv_prep(zr_s, first_s, ts, 256, rw)
    or_s, s_s = _rwkv_scan(prep_s, lnw, lnb, avg, _block_diag_state(state_wkv[l]),
                           chunk=ts, carry=False)
    mods_s = (mod_rows(0, 2, False), mod_rows(1, 0, False), mod_rows(1, 1, False), mod_rows(1, 2, False))
    y_s = _ffn(xs, oa_s.reshape(n_s, W_A), or_s, mods_s, g2, gf, wo_bf, wr, br, wg_bf, wu_bf, wd_bf,
               tm=512, oa_transposed=False)

    return (
        y_p.reshape(bp, sp, D_MODEL),
        y_s.reshape(bs, ts, D_MODEL),
        k_p.reshape(1, bp, sp, H_A, 2, DK_A),
        v_p.reshape(1, bp, sp, H_A, DV_A),
        _diag_state(s_p[None]).astype(state_wkv.dtype)[None],
        zr_p[n_p - 1:].reshape(1, bp, C_R),
        k_s.reshape(1, bs, ts, H_A, 2, DK_A),
        v_s.reshape(1, bs, ts, H_A, DV_A),
        _diag_state(s_s).astype(state_wkv.dtype)[None],
        zr_s.reshape(bs, ts, C_R)[:, -1][None],
    )
```

```python
import functools
import math

import jax
import jax.numpy as jnp
from jax import lax
from jax.experimental import pallas as pl
from jax.experimental.pallas import tpu as pltpu

F32 = jnp.float32
BF16 = jnp.bfloat16

D_MODEL = 1024
PAGE_SIZE = 128
H_A = 8
DK_A = 32
DV_A = 64
H_R = 8
HD_R = 64
D_DECAY_LORA = 64
D_AAA_LORA = 64
D_GATE_LORA = 128
N_GROUPS = 4
EXPERTS_PER_GROUP = 4
N_EXPERTS = N_GROUPS * EXPERTS_PER_GROUP
D_EXPERT = 512
NORM_EPS = 1e-6
LN_X_EPS = 64e-5
NEG_INF = -1e30
W_A = H_A * DV_A
W_R = H_R * HD_R
C_QA = H_A * 2 * DK_A
C_A = 2 * C_QA + W_A
C_R = 3 * W_R + D_DECAY_LORA + D_AAA_LORA + D_GATE_LORA
LAM_INIT = 0.8 - 0.6 * math.exp(-0.3 * 0)

LANES = 128
LOG2E = 1.4426950408889634
Q_SCALE = DK_A ** -0.5 * LOG2E
VMEM_LIMIT = 56 * 1024 * 1024


def _dot(a, b):
    return jnp.dot(a, b, preferred_element_type=F32)


def _dot_nt(a, b):
    return lax.dot_general(a, b, (((1,), (1,)), ((), ())), preferred_element_type=F32)


def _dot_tn(a, b):
    return lax.dot_general(a, b, (((0,), (0,)), ((), ())), preferred_element_type=F32)


def _split3(x):
    hi = x.astype(BF16)
    r1 = x - hi.astype(F32)
    mid = r1.astype(BF16)
    lo = (r1 - mid.astype(F32)).astype(BF16)
    return hi, mid, lo


def _dot_exact_rhs(x, m_bf16):
    hi, mid, lo = _split3(x)
    return _dot(hi, m_bf16) + _dot(mid, m_bf16) + _dot(lo, m_bf16)


def _dot_exact_lhs(m_bf16, x):
    hi, mid, lo = _split3(x)
    return _dot(m_bf16, hi) + _dot(m_bf16, mid) + _dot(m_bf16, lo)


def _dot_f32(a, b):
    ah = a.astype(BF16)
    al = (a - ah.astype(F32)).astype(BF16)
    bh = b.astype(BF16)
    bl = (b - bh.astype(F32)).astype(BF16)
    return _dot(ah, bh) + _dot(ah, bl) + _dot(al, bh)


def _sigmoid(x):
    return 1.0 / (1.0 + jnp.exp(-x))


def _params(*sem):
    return pltpu.CompilerParams(dimension_semantics=sem, vmem_limit_bytes=VMEM_LIMIT)


def _ada_kernel(c_ref, w_ref, b_ref, o_ref):
    c = c_ref[...]
    s = (c * _sigmoid(c)).astype(BF16)
    o_ref[0] = _dot(s, w_ref[0]) + b_ref[0]


def _ada_mod(c_all, ada_w, ada_b):
    r = c_all.shape[0]
    n_col = 3
    return pl.pallas_call(
        _ada_kernel,
        grid=(2, n_col),
        in_specs=[
            pl.BlockSpec((r, D_MODEL), lambda s, j: (0, 0)),
            pl.BlockSpec((1, D_MODEL, D_MODEL), lambda s, j: (s, 0, j)),
            pl.BlockSpec((1, 1, D_MODEL), lambda s, j: (s, 0, j)),
        ],
        out_specs=pl.BlockSpec((1, r, D_MODEL), lambda s, j: (s, 0, j)),
        out_shape=jax.ShapeDtypeStruct((2, r, 3 * D_MODEL), F32),
        compiler_params=_params("arbitrary", "arbitrary"),
        name="ada_mod",
    )(c_all, ada_w, ada_b.reshape(2, 1, 3 * D_MODEL))


def _modulated_norm(x, g, sc, sh):
    ms = jnp.mean(x * x, axis=-1, keepdims=True)
    return (x * lax.rsqrt(ms + NORM_EPS) * g) * (1.0 + sc) + sh


def _inproj_kernel(x_ref, sh_ref, sc_ref, g_ref, w_ref, wt_ref,
                   q_ref, kb_ref, vb_ref, k_ref, v_ref, zr_ref, *, transposed):
    hb = _modulated_norm(x_ref[...], g_ref[...], sc_ref[...], sh_ref[...]).astype(BF16)
    zk = _dot(hb, w_ref[:, C_QA:2 * C_QA])
    k_ref[...] = zk
    kb_ref[...] = zk.astype(BF16)
    zv = _dot(hb, w_ref[:, 2 * C_QA:C_A])
    v_ref[...] = zv
    zr_ref[...] = _dot(hb, w_ref[:, C_A:])
    if transposed:
        q_ref[...] = (_dot_nt(wt_ref[0:C_QA], hb) * Q_SCALE).astype(BF16)
        vb_ref[...] = _dot_nt(wt_ref[C_QA:C_QA + W_A], hb).astype(BF16)
    else:
        q_ref[...] = (_dot(hb, w_ref[:, 0:C_QA]) * Q_SCALE).astype(BF16)
        vb_ref[...] = zv.astype(BF16)


def _in_proj(x, sh, sc, g, w_bf, wt_bf, *, tm, transposed):
    n = x.shape[0]
    per_tok = sh.shape[0] != 1
    mod_spec = (pl.BlockSpec((tm, D_MODEL), lambda i: (i, 0)) if per_tok
                else pl.BlockSpec((1, D_MODEL), lambda i: (0, 0)))
    tok = lambda w: pl.BlockSpec((tm, w), lambda i: (i, 0))
    if transposed:
        qv_spec = pl.BlockSpec((C_QA, tm), lambda i: (0, i))
        qv_shape = jax.ShapeDtypeStruct((C_QA, n), BF16)
    else:
        qv_spec = tok(C_QA)
        qv_shape = jax.ShapeDtypeStruct((n, C_QA), BF16)
    return pl.pallas_call(
        functools.partial(_inproj_kernel, transposed=transposed),
        grid=(n // tm,),
        in_specs=[
            tok(D_MODEL), mod_spec, mod_spec,
            pl.BlockSpec((1, D_MODEL), lambda i: (0, 0)),
            pl.BlockSpec(w_bf.shape, lambda i: (0, 0)),
            pl.BlockSpec(wt_bf.shape, lambda i: (0, 0)),
        ],
        out_specs=[qv_spec, tok(C_QA), qv_spec, tok(C_QA), tok(W_A), tok(C_R)],
        out_shape=[
            qv_shape,
            jax.ShapeDtypeStruct((n, C_QA), BF16),
            qv_shape,
            jax.ShapeDtypeStruct((n, C_QA), F32),
            jax.ShapeDtypeStruct((n, W_A), F32),
            jax.ShapeDtypeStruct((n, C_R), F32),
        ],
        compiler_params=_params("arbitrary"),
        name="in_proj_t" if transposed else "in_proj",
    )(x, sh, sc, g, w_bf, wt_bf)


def _lambda(q1_ref, k1_ref, q2_ref, k2_ref):
    l1 = jnp.sum(q1_ref[...] * k1_ref[...], axis=-1, keepdims=True)
    l2 = jnp.sum(q2_ref[...] * k2_ref[...], axis=-1, keepdims=True)
    return jnp.exp(l1) - jnp.exp(l2) + LAM_INIT


def _attn_prompt_kernel(qt_ref, k_ref, vt_ref, g_ref, lq1, lk1, lq2, lk2, o_ref,
                        m_scr, l_scr, acc_scr, *, tq, tk):
    i = pl.program_id(1)
    n_sub = tq // tk
    m_scr[...] = jnp.full(m_scr.shape, NEG_INF, F32)
    l_scr[...] = jnp.zeros(l_scr.shape, F32)
    acc_scr[...] = jnp.zeros(acc_scr.shape, F32)

    qt = qt_ref[...]
    row_q = lax.broadcasted_iota(jnp.int32, (LANES, tq), 0)
    qtm = [jnp.where((row_q >= DK_A * i4) & (row_q < DK_A * (i4 + 1)), qt, jnp.zeros_like(qt))
           for i4 in range(4)]
    row_v = lax.broadcasted_iota(jnp.int32, (LANES, tk), 0)
    first_head = row_q < DV_A

    def step(j, diag_offset):
        start = pl.multiple_of(j * tk, tk)
        kblk = k_ref[pl.ds(start, tk), :]
        vtblk = vt_ref[:, pl.ds(start, tk)]
        if diag_offset is not None:
            visible = (lax.broadcasted_iota(jnp.int32, (tk, tq), 0) + diag_offset
                       <= lax.broadcasted_iota(jnp.int32, (tk, tq), 1))
        pts, alphas = [], []
        for i4 in range(4):
            st = _dot(kblk, qtm[i4])
            if diag_offset is not None:
                st = jnp.where(visible, st, NEG_INF)
            m_prev = m_scr[i4]
            m_new = jnp.maximum(m_prev, jnp.max(st, axis=0, keepdims=True))
            alpha = jnp.exp2(m_prev - m_new)
            pt = jnp.exp2(st - m_new)
            l_scr[i4] = alpha * l_scr[i4] + jnp.sum(pt, axis=0, keepdims=True)
            m_scr[i4] = m_new
            pts.append(pt.astype(BF16))
            alphas.append(alpha)
        vbd = jnp.concatenate(
            [jnp.where(row_v < DV_A, vtblk, jnp.zeros_like(vtblk)),
             jnp.where(row_v >= DV_A, vtblk, jnp.zeros_like(vtblk))], axis=1)
        for m in range(2):
            pcat = jnp.concatenate([pts[m], pts[2 + m]], axis=0)
            alpha_m = jnp.where(first_head, alphas[m], alphas[2 + m])
            acc_scr[m] = acc_scr[m] * alpha_m + _dot(vbd, pcat)

    def body(jj, carry):
        for u in range(n_sub):
            step(jj * n_sub + u, None)
        return carry

    lax.fori_loop(0, i, body, 0)
    for u in range(n_sub):
        step(i * n_sub + u, u * tk)

    lam = _lambda(lq1, lk1, lq2, lk2)
    o0 = acc_scr[0] / jnp.where(first_head, l_scr[0], l_scr[2])
    o1 = acc_scr[1] / jnp.where(first_head, l_scr[1], l_scr[3])
    o = (o0 - lam * o1).reshape(2, DV_A, tq)
    ms = jnp.mean(o * o, axis=1, keepdims=True)
    o = o * lax.rsqrt(ms + NORM_EPS)
    o = o.reshape(LANES, tq) * g_ref[:, 0:1] * (1.0 - LAM_INIT)
    o_ref[...] = o.astype(o_ref.dtype)


def _attn_prompt(qt, k_bf, vt, g_col, lams, *, tq, tk):
    s = qt.shape[1]
    lam_spec = pl.BlockSpec((1, DK_A), lambda p, i: (0, 0))
    return pl.pallas_call(
        functools.partial(_attn_prompt_kernel, tq=tq, tk=tk),
        grid=(H_A // 2, s // tq),
        in_specs=[
            pl.BlockSpec((LANES, tq), lambda p, i: (p, i)),
            pl.BlockSpec((s, LANES), lambda p, i: (0, p)),
            pl.BlockSpec((LANES, s), lambda p, i: (p, 0)),
            pl.BlockSpec((LANES, LANES), lambda p, i: (0, 0)),
            lam_spec, lam_spec, lam_spec, lam_spec,
        ],
        out_specs=pl.BlockSpec((LANES, tq), lambda p, i: (p, i)),
        out_shape=jax.ShapeDtypeStruct((W_A, s), BF16),
        scratch_shapes=[
            pltpu.VMEM((4, 1, tq), F32),
            pltpu.VMEM((4, 1, tq), F32),
            pltpu.VMEM((2, LANES, tq), F32),
        ],
        compiler_params=_params("arbitrary", "arbitrary"),
        name="attn_prompt",
    )(qt, k_bf, vt, g_col, *lams)


PAGES_PER_STEP = 16
SEQS_PER_STEP = 2


def _attn_sample_kernel(pt_ref, qbd_ref, kn_ref, vn_ref, g_ref, seg_ref, lq1, lk1, lq2, lk2, *rest,
                        n_steps, t_new):
    n_pg = SEQS_PER_STEP * PAGES_PER_STEP
    k_refs, v_refs = rest[:n_pg], rest[n_pg:2 * n_pg]
    o_ref, m_scr, l_scr, acc_scr = rest[2 * n_pg:]
    c = pl.program_id(1)

    @pl.when(c == 0)
    def _():
        m_scr[...] = jnp.full(m_scr.shape, NEG_INF, F32)
        l_scr[...] = jnp.zeros(l_scr.shape, F32)
        acc_scr[...] = jnp.zeros(acc_scr.shape, F32)

    def update(q, s, pv):
        m_prev = m_scr[q]
        m_new = jnp.maximum(m_prev, jnp.max(s, axis=1, keepdims=True))
        alpha = jnp.exp2(m_prev - m_new)
        p = jnp.exp2(s - m_new)
        l_scr[q] = alpha * l_scr[q] + jnp.sum(p, axis=1, keepdims=True)
        m_scr[q] = m_new
        acc_scr[q] = acc_scr[q] * alpha + pv(p.astype(BF16))

    for q in range(SEQS_PER_STEP):
        pages = slice(q * PAGES_PER_STEP, (q + 1) * PAGES_PER_STEP)
        kt = jnp.concatenate([r[0] for r in k_refs[pages]], axis=1).astype(BF16)
        vt = jnp.concatenate([r[0] for r in v_refs[pages]], axis=1).astype(BF16)
        update(q, _dot(qbd_ref[q], kt), lambda p, vt=vt: _dot_nt(p, vt))

    @pl.when(c == n_steps - 1)
    def _():
        lam = _lambda(lq1, lk1, lq2, lk2)
        pad = jnp.zeros((16 - t_new, C_QA), F32)
        trow = lax.broadcasted_iota(jnp.int32, (LANES, 16), 0) % t_new
        jcol = lax.broadcasted_iota(jnp.int32, (LANES, 16), 1)
        lane_head = lax.broadcasted_iota(jnp.int32, (t_new, W_A), 1) // DV_A
        for q in range(SEQS_PER_STEP):
            kn = jnp.concatenate([kn_ref[q], pad], axis=0).astype(BF16)
            vn = jnp.concatenate([vn_ref[q], pad], axis=0).astype(BF16)
            s_new = jnp.where(jcol <= trow, _dot_nt(qbd_ref[q], kn), NEG_INF)
            update(q, s_new, lambda p, vn=vn: _dot(p, vn))

            accn = acc_scr[q] / l_scr[q]
            outs = []
            for m in range(2):
                o_m = jnp.zeros((t_new, W_A), F32)
                for h in range(H_A):
                    r0 = h * 2 * t_new + m * t_new
                    o_m = jnp.where(lane_head == h, accn[r0:r0 + t_new, :], o_m)
                outs.append(o_m)
            o = outs[0] - lam * outs[1]
            ms = _dot_exact_rhs(o * o, seg_ref[...]) * (1.0 / DV_A)
            o_ref[q] = o * lax.rsqrt(ms + NORM_EPS) * g_ref[...] * (1.0 - LAM_INIT)


def _attn_sample(page_table, qbd, k_new, v_new, kt_pages, vt_pages, g_row, seg, lams):
    b, t_new = k_new.shape[0], k_new.shape[1]
    n_pages = page_table.shape[1]
    n_steps = n_pages // PAGES_PER_STEP
    nq = SEQS_PER_STEP
    pt_flat = page_table.reshape(-1)

    def page_spec(q, j):
        return pl.BlockSpec(
            (1, C_QA, PAGE_SIZE),
            lambda bi, c, pt: (pt[(bi * nq + q) * n_pages + c * PAGES_PER_STEP + j], 0, 0))

    page_specs = [page_spec(q, j) for q in range(nq) for j in range(PAGES_PER_STEP)]
    lam_spec = pl.BlockSpec((1, DK_A), lambda bi, c, pt: (0, 0))
    seq = lambda shp: pl.BlockSpec(shp, lambda bi, c, pt: (bi, 0, 0))
    grid_spec = pltpu.PrefetchScalarGridSpec(
        num_scalar_prefetch=1,
        grid=(b // nq, n_steps),
        in_specs=[
            seq((nq, LANES, C_QA)), seq((nq, t_new, C_QA)), seq((nq, t_new, W_A)),
            pl.BlockSpec((1, W_A), lambda bi, c, pt: (0, 0)),
            pl.BlockSpec((W_A, W_A), lambda bi, c, pt: (0, 0)),
            lam_spec, lam_spec, lam_spec, lam_spec,
        ] + page_specs * 2,
        out_specs=seq((nq, t_new, W_A)),
        scratch_shapes=[
            pltpu.VMEM((nq, LANES, 1), F32),
            pltpu.VMEM((nq, LANES, 1), F32),
            pltpu.VMEM((nq, LANES, W_A), F32),
        ],
    )
    return pl.pallas_call(
        functools.partial(_attn_sample_kernel, n_steps=n_steps, t_new=t_new),
        grid_spec=grid_spec,
        out_shape=jax.ShapeDtypeStruct((b, t_new, W_A), F32),
        compiler_params=_params("arbitrary", "arbitrary"),
        name="attn_sample",
    )(pt_flat, qbd, k_new, v_new, g_row, seg, *lams,
      *([kt_pages] * (nq * PAGES_PER_STEP)), *([vt_pages] * (nq * PAGES_PER_STEP)))


def _rwkv_prep_kernel(zr_ref, first_ref, mu_ref, w0_ref, w2_ref, a0_ref, a2_ref, g2_ref,
                      kk_ref, ka_ref, rk_ref, seg_ref,
                      r_o, lw_o, k_o, v_o, kn_o, b_o, g_o, bonus_o, *, period):
    z = zr_ref[...]
    row = lax.broadcasted_iota(jnp.int32, z.shape, 0)
    if first_ref.shape[0] == z.shape[0]:
        first = first_ref[...]
    else:
        first = jnp.where(pl.program_id(0) == 0, 0.0, first_ref[7:8, :])
    z_prev = jnp.where((row & (period - 1)) == 0, first, pltpu.roll(z, 1, 0))
    z = z + (z_prev - z) * mu_ref[...]
    o1, o2, o3 = W_R, 2 * W_R, 3 * W_R
    o5 = o3 + D_DECAY_LORA + D_AAA_LORA
    r, k, v = z[:, :o1], z[:, o1:o2], z[:, o2:o3]
    zwa = z[:, o3:o5]
    zg = z[:, o5:]
    w_pre = w0_ref[...] + _dot_f32(jnp.tanh(zwa), w2_ref[...])
    w_raw = -(jnp.maximum(-w_pre, 0.0) + jnp.log(1.0 + jnp.exp(-jnp.abs(w_pre)))) - 0.5
    lw_o[...] = -jnp.exp(w_raw)
    a = _sigmoid(a0_ref[...] + _dot(zwa.astype(BF16), a2_ref[...]))
    g_o[...] = _dot(_sigmoid(zg).astype(BF16), g2_ref[...])
    kk = k * kk_ref[...]
    ssq = _dot_exact_rhs(kk * kk, seg_ref[...])
    kn = kk * lax.rsqrt(ssq + 1e-12)
    k2 = k * (1.0 + (a - 1.0) * ka_ref[...])
    rk = _dot_exact_rhs(r * k2 * rk_ref[...], seg_ref[...])
    r_o[...] = r
    k_o[...] = k2
    v_o[...] = v
    kn_o[...] = kn
    b_o[...] = kn * a
    bonus_o[...] = rk * v


def _rwkv_prep(zr, first, period, tm, p):
    n = zr.shape[0]
    assert period & (period - 1) == 0
    tok = lambda w: pl.BlockSpec((tm, w), lambda i: (i, 0))
    full = lambda a: pl.BlockSpec(a.shape, lambda i: (0,) * a.ndim)
    if first is None:
        first, first_spec = zr, pl.BlockSpec((8, C_R), lambda i: (jnp.maximum(i * (tm // 8) - 1, 0), 0))
    else:
        first_spec = tok(C_R)
    consts = [p["mu"], p["w0"], p["w2"], p["a0"], p["a2"], p["g2"], p["kk"], p["ka"], p["rk"], p["seg"]]
    return pl.pallas_call(
        functools.partial(_rwkv_prep_kernel, period=period),
        grid=(n // tm,),
        in_specs=[tok(C_R), first_spec] + [full(a) for a in consts],
        out_specs=[tok(W_R)] * 8,
        out_shape=[jax.ShapeDtypeStruct((n, W_R), F32)] * 8,
        compiler_params=_params("arbitrary"),
        name="rwkv_prep",
    )(zr, first, *consts)


SUPER = 256


def _rwkv_scan_kernel(r_ref, lw_ref, k_ref, v_ref, kn_ref, b_ref, g_ref, bonus_ref,
                      lnw_ref, lnb_ref, avg_ref, s_in_ref, o_ref, s_out_ref, *, chunk, carry):
    n_blk = SUPER // chunk
    n_pair = H_R // 2
    shift = chunk.bit_length() - 1
    ri = lax.broadcasted_iota(jnp.int32, (SUPER, SUPER), 0)
    ci = lax.broadcasted_iota(jnp.int32, (SUPER, SUPER), 1)
    same = (ri >> shift) == (ci >> shift)
    strict = same & (ci < ri)
    incl = same & (ci <= ri)
    m_incl = jnp.where(incl, 1.0, 0.0).astype(BF16)

    if carry:
        @pl.when(pl.program_id(0) == 0)
        def _():
            s_out_ref[...] = s_in_ref[...]

    lw = lw_ref[...]
    cum = _dot_exact_lhs(m_incl, lw)
    gi = jnp.exp(cum)
    a_t = -kn_ref[...] * jnp.exp(cum - lw)
    ginv = jnp.exp(-cum)
    b_t = b_ref[...] * ginv
    k_t = k_ref[...] * ginv
    r_t = r_ref[...] * gi
    v_all = v_ref[...]

    lane = lax.broadcasted_iota(jnp.int32, (SUPER, LANES), 1)
    bd = ((lax.broadcasted_iota(jnp.int32, (LANES, LANES), 0) // HD_R)
          == (lax.broadcasted_iota(jnp.int32, (LANES, LANES), 1) // HD_R))
    col_blk = lax.broadcasted_iota(jnp.int32, (LANES, SUPER), 1) >> shift

    pre = []
    for p in range(n_pair):
        ls = slice(p * LANES, (p + 1) * LANES)
        a_p, b_p, k_p, r_p, v_p = a_t[:, ls], b_t[:, ls], k_t[:, ls], r_t[:, ls], v_all[:, ls]
        b_pb, k_pb = b_p.astype(BF16), k_p.astype(BF16)
        a2_p = jnp.zeros((SUPER, LANES), F32)
        p0_p = jnp.zeros((SUPER, LANES), F32)
        y0_p = jnp.zeros((SUPER, LANES), F32)
        wrbs = []
        for hh in range(2):
            hm = (lane // HD_R) == hh
            a_m = jnp.where(hm, a_p, 0.0)
            a_mb = a_m.astype(BF16)
            r_mb = jnp.where(hm, r_p, 0.0).astype(BF16)
            v_mb = jnp.where(hm, v_p, 0.0).astype(BF16)
            l_ab = jnp.where(strict, _dot_nt(a_mb, b_pb), 0.0)
            l_ak = jnp.where(strict, _dot_nt(a_mb, k_pb), 0.0)
            w_rb = jnp.where(incl, _dot_nt(r_mb, b_pb), 0.0)
            w_rk = jnp.where(incl, _dot_nt(r_mb, k_pb), 0.0)
            t_m = l_ab
            l_pow = l_ab
            for _ in range(shift - 1):
                lb = l_pow.astype(BF16)
                l_pow = _dot(lb, lb)
                t_m = t_m + l_pow + _dot(t_m.astype(BF16), l_pow.astype(BF16))
            t_mb = t_m.astype(BF16)
            a2_p = a2_p + a_m + _dot(t_mb, a_mb)
            u0 = _dot(l_ak.astype(BF16), v_mb)
            p0_p = p0_p + u0 + _dot(t_mb, u0.astype(BF16))
            y0_p = y0_p + _dot(w_rk.astype(BF16), v_mb)
            wrbs.append(w_rb.astype(BF16))
        pre.append(dict(
            ar=jnp.concatenate([a2_p, r_p], axis=0).astype(BF16),
            bk=jnp.concatenate([b_pb, k_pb], axis=0),
            p0t=jnp.transpose(p0_p),
            vt=jnp.transpose(v_p).astype(BF16),
            y0=y0_p, wrbs=wrbs))

    states = [s_out_ref[p] for p in range(n_pair)] if carry else [None] * n_pair
    pt_acc = [jnp.zeros((LANES, SUPER), F32) for _ in range(n_pair)]
    yrt_acc = [jnp.zeros((LANES, SUPER), F32) for _ in range(n_pair)]
    for c in range(n_blk):
        in_blk = col_blk == c
        for p in range(n_pair):
            q = pre[p]
            s = states[p] if carry else s_in_ref[c, p]
            out = _dot_nt(s.astype(BF16), q["ar"])
            ptm = jnp.where(in_blk, out[:, :SUPER] + q["p0t"], 0.0)
            pt_acc[p] = pt_acc[p] + ptm
            yrt_acc[p] = jnp.where(in_blk, out[:, SUPER:], yrt_acc[p])
            lhs = jnp.concatenate([ptm.astype(BF16), jnp.where(in_blk, q["vt"], jnp.zeros_like(q["vt"]))], axis=1)
            upd = _dot(lhs, q["bk"])
            g_end = gi[(c + 1) * chunk - 1:(c + 1) * chunk, p * LANES:(p + 1) * LANES]
            s = (s + jnp.where(bd, upd, 0.0)) * g_end
            if carry:
                states[p] = s
            else:
                s_out_ref[c, p] = s

    for p in range(n_pair):
        ls = slice(p * LANES, (p + 1) * LANES)
        q = pre[p]
        if carry:
            s_out_ref[p] = states[p]
        p_full = jnp.transpose(pt_acc[p])
        y = jnp.transpose(yrt_acc[p]) + q["y0"]
        for hh in range(2):
            hm = (lane // HD_R) == hh
            y = y + _dot(q["wrbs"][hh], jnp.where(hm, p_full, 0.0).astype(BF16))
        mean = _dot_exact_rhs(y, avg_ref[...])
        yc = y - mean
        var = _dot_exact_rhs(yc * yc, avg_ref[...])
        yn = yc * lax.rsqrt(var + LN_X_EPS) * lnw_ref[:, ls] + lnb_ref[:, ls]
        o_ref[:, ls] = ((yn + bonus_ref[:, ls]) * g_ref[:, ls]).astype(o_ref.dtype)


def _rwkv_scan(prep, lnw, lnb, avg, s_in, *, chunk, carry):
    n = prep[0].shape[0]
    n_blk = SUPER // chunk
    tok = pl.BlockSpec((SUPER, W_R), lambda i: (i, 0))
    row = pl.BlockSpec((1, W_R), lambda i: (0, 0))
    if carry:
        s_spec = pl.BlockSpec(s_in.shape, lambda i: (0, 0, 0))
    else:
        s_spec = pl.BlockSpec((n_blk,) + s_in.shape[1:], lambda i: (i, 0, 0, 0))
    return pl.pallas_call(
        functools.partial(_rwkv_scan_kernel, chunk=chunk, carry=carry),
        grid=(n // SUPER,),
        in_specs=[tok] * 8 + [row, row, pl.BlockSpec((LANES, LANES), lambda i: (0, 0)), s_spec],
        out_specs=[tok, s_spec],
        out_shape=[jax.ShapeDtypeStruct((n, W_R), BF16), jax.ShapeDtypeStruct(s_in.shape, F32)],
        compiler_params=_params("arbitrary"),
        name="rwkv_scan_carry" if carry else "rwkv_scan_seq",
    )(*prep, lnw, lnb, avg, s_in)


ROUTE_W = 128
EXPERT_LANE0 = 8


def _route(logits):
    lane = lax.broadcasted_iota(jnp.int32, logits.shape, 1)
    big = jnp.int32(1 << 20)
    g_log = jnp.where(lane < N_GROUPS, logits, -jnp.inf)
    g_max = jnp.max(g_log, axis=-1, keepdims=True)
    g_idx = jnp.min(jnp.where(g_log == g_max, lane, big), axis=-1, keepdims=True)
    g_w = 1.0 / jnp.sum(jnp.exp(g_log - g_max), axis=-1, keepdims=True)
    e_lo = EXPERT_LANE0 + g_idx * EXPERTS_PER_GROUP
    e_log = jnp.where((lane >= e_lo) & (lane < e_lo + EXPERTS_PER_GROUP), logits, -jnp.inf)
    e_max = jnp.max(e_log, axis=-1, keepdims=True)
    i1 = jnp.min(jnp.where(e_log == e_max, lane, big), axis=-1, keepdims=True)
    e_log2 = jnp.where(lane == i1, -jnp.inf, e_log)
    e_max2 = jnp.max(e_log2, axis=-1, keepdims=True)
    i2 = jnp.min(jnp.where(e_log2 == e_max2, lane, big), axis=-1, keepdims=True)
    p2 = jnp.exp(e_max2 - e_max)
    w1 = g_w / (1.0 + p2)
    w2 = g_w * p2 / (1.0 + p2)
    return jnp.where(lane == i1, w1, 0.0) + jnp.where(lane == i2, w2, 0.0)


def _ffn_kernel(x_ref, oa_ref, or_ref, gt1_ref, sh_ref, sc_ref, gt2_ref, g2_ref, gf_ref,
                wo_ref, wr_ref, br_ref, wg_ref, wu_ref, wd_ref, y_ref,
                x1_scr, h_scr, comb_scr, acc_scr, *, oa_transposed):
    e = pl.program_id(1)

    @pl.when(e == 0)
    def _():
        if oa_transposed:
            mix = _dot_tn(oa_ref[...], wo_ref[0:W_A])
        else:
            mix = _dot(oa_ref[...].astype(BF16), wo_ref[0:W_A])
        mix = mix + _dot(or_ref[...], wo_ref[W_A:])
        x1 = x_ref[...] + gt1_ref[...] * mix
        x1_scr[...] = x1
        h = _modulated_norm(x1, g2_ref[...], sc_ref[...], sh_ref[...])
        h_scr[...] = h.astype(BF16)
        comb_scr[...] = _route(_dot_f32(h, wr_ref[...]) + br_ref[...])
        acc_scr[...] = jnp.zeros(acc_scr.shape, F32)

    hb = h_scr[...]
    hg = _dot(hb, wg_ref[0])
    hu = _dot(hb, wu_ref[0])
    lane = lax.broadcasted_iota(jnp.int32, comb_scr.shape, 1)
    comb_e = jnp.sum(jnp.where(lane == EXPERT_LANE0 + e, comb_scr[...], 0.0), axis=-1, keepdims=True)
    act = (hg * _sigmoid(hg)) * hu * comb_e
    acc_scr[...] += _dot(act.astype(BF16), wd_ref[0])

    @pl.when(e == N_EXPERTS - 1)
    def _():
        x2 = x1_scr[...] + gt2_ref[...] * acc_scr[...]
        ms = jnp.mean(x2 * x2, axis=-1, keepdims=True)
        y_ref[...] = x2 * lax.rsqrt(ms + NORM_EPS) * gf_ref[...]


def _ffn(x, oa, o_r, mods, g2, gf, wo_bf, wr, br, wg_bf, wu_bf, wd_bf, *, tm, oa_transposed):
    n = x.shape[0]
    per_tok = mods[0].shape[0] != 1
    mod_spec = (pl.BlockSpec((tm, D_MODEL), lambda i, e: (i, 0)) if per_tok
                else pl.BlockSpec((1, D_MODEL), lambda i, e: (0, 0)))
    tok = lambda w: pl.BlockSpec((tm, w), lambda i, e: (i, 0))
    oa_spec = pl.BlockSpec((W_A, tm), lambda i, e: (0, i)) if oa_transposed else tok(W_A)
    row = pl.BlockSpec((1, D_MODEL), lambda i, e: (0, 0))
    return pl.pallas_call(
        functools.partial(_ffn_kernel, oa_transposed=oa_transposed),
        grid=(n // tm, N_EXPERTS),
        in_specs=[
            tok(D_MODEL), oa_spec, tok(W_R), mod_spec, mod_spec, mod_spec, mod_spec, row, row,
            pl.BlockSpec(wo_bf.shape, lambda i, e: (0, 0)),
            pl.BlockSpec(wr.shape, lambda i, e: (0, 0)),
            pl.BlockSpec((1, ROUTE_W), lambda i, e: (0, 0)),
            pl.BlockSpec((1, D_MODEL, D_EXPERT), lambda i, e: (e, 0, 0)),
            pl.BlockSpec((1, D_MODEL, D_EXPERT), lambda i, e: (e, 0, 0)),
            pl.BlockSpec((1, D_EXPERT, D_MODEL), lambda i, e: (e, 0, 0)),
        ],
        out_specs=tok(D_MODEL),
        out_shape=jax.ShapeDtypeStruct((n, D_MODEL), F32),
        scratch_shapes=[
            pltpu.VMEM((tm, D_MODEL), F32),
            pltpu.VMEM((tm, D_MODEL), BF16),
            pltpu.VMEM((tm, ROUTE_W), F32),
            pltpu.VMEM((tm, D_MODEL), F32),
        ],
        compiler_params=_params("arbitrary", "arbitrary"),
        name="ffn_t" if oa_transposed else "ffn",
    )(x, oa, o_r, *mods, g2, gf, wo_bf, wr, br, wg_bf, wu_bf, wd_bf)


def _block_diag_state(s):
    b = s.shape[0]
    s = s.reshape(b, H_R // 2, 2, HD_R, HD_R)
    z = jnp.zeros_like(s[:, :, 0])
    top = jnp.concatenate([s[:, :, 0], z], axis=-1)
    bot = jnp.concatenate([z, s[:, :, 1]], axis=-1)
    return jnp.concatenate([top, bot], axis=-2)


def _diag_state(sbd):
    b = sbd.shape[0]
    h0 = sbd[:, :, :HD_R, :HD_R]
    h1 = sbd[:, :, HD_R:, HD_R:]
    return jnp.stack([h0, h1], axis=2).reshape(b, H_R, HD_R, HD_R)


def kernel(x_prompt, x_sample, c_prompt, c_sample, cache_k, cache_v, state_wkv, state_shift, page_table, ada_w, ada_b, norm_mix_g, norm_ffn_g, w_in, w_out, lam_q1, lam_k1, lam_q2, lam_k2, subln_g, rw_mu, rw_w0, rw_w2, rw_a0, rw_a2, rw_g2, rw_kk, rw_ka, rw_rk, rw_lnw, rw_lnb, moe_wg, moe_bg, moe_we, moe_be, moe_w_gate, moe_w_up, moe_w_down, final_g):
    bp, sp, _ = x_prompt.shape
    bs, ts, _ = x_sample.shape
    assert bp == 1 and ada_w.shape[0] == 1, "single prompt row and single layer only"
    n_p, n_s = bp * sp, bs * ts
    l = 0

    w_bf = w_in[l].astype(BF16)
    wt_bf = jnp.concatenate([w_in[l][:, :C_QA].T, w_in[l][:, 2 * C_QA:C_A].T], axis=0).astype(BF16)
    wo_bf = w_out[l].astype(BF16)
    lams = [a[l].reshape(1, DK_A) for a in (lam_q1, lam_k1, lam_q2, lam_k2)]
    g_col = jnp.broadcast_to(jnp.tile(subln_g[l], 2)[:, None], (LANES, LANES))
    g_row = jnp.tile(subln_g[l], H_A).reshape(1, W_A)
    head_of = jnp.arange(W_R) // HD_R
    seg = (head_of[:, None] == head_of[None, :]).astype(BF16)
    avg = (seg[:LANES, :LANES].astype(F32) / HD_R).astype(BF16)
    zpad = jnp.zeros((D_DECAY_LORA, W_R), F32)
    rw = {
        "mu": rw_mu[l].reshape(1, C_R), "w0": rw_w0[l].reshape(1, W_R),
        "w2": jnp.concatenate([rw_w2[l], zpad], axis=0),
        "a0": rw_a0[l].reshape(1, W_R),
        "a2": jnp.concatenate([zpad, rw_a2[l]], axis=0).astype(BF16),
        "g2": rw_g2[l].astype(BF16),
        "kk": rw_kk[l].reshape(1, W_R), "ka": rw_ka[l].reshape(1, W_R),
        "rk": rw_rk[l].reshape(1, W_R), "seg": seg,
    }
    lnw, lnb = rw_lnw[l].reshape(1, W_R), rw_lnb[l].reshape(1, W_R)
    wr = jnp.zeros((D_MODEL, ROUTE_W), F32)
    wr = wr.at[:, :N_GROUPS].set(moe_wg[l]).at[:, EXPERT_LANE0:EXPERT_LANE0 + N_EXPERTS].set(moe_we[l])
    br = jnp.zeros((1, ROUTE_W), F32)
    br = br.at[0, :N_GROUPS].set(moe_bg[l]).at[0, EXPERT_LANE0:EXPERT_LANE0 + N_EXPERTS].set(moe_be[l])
    wg_bf, wu_bf, wd_bf = (a[l].astype(BF16) for a in (moe_w_gate, moe_w_up, moe_w_down))
    g1 = norm_mix_g[l].reshape(1, D_MODEL)
    g2 = norm_ffn_g[l].reshape(1, D_MODEL)
    gf = final_g.reshape(1, D_MODEL)

    r_pad = -(bs + bp) % 8
    c_all = jnp.concatenate([c_sample, c_prompt, jnp.zeros((r_pad, D_MODEL), F32)], axis=0)
    mods = _ada_mod(c_all, ada_w[l].astype(BF16), ada_b[l])

    def mod_rows(sub, part, prompt):
        m = mods[sub, :, part * D_MODEL:(part + 1) * D_MODEL]
        if prompt:
            return m[bs:bs + 1]
        return jnp.repeat(m[:bs], ts, axis=0)

    xp = x_prompt.reshape(n_p, D_MODEL)
    qt, kb, vt, k_p, v_p, zr_p = _in_proj(
        xp, mod_rows(0, 0, True), mod_rows(0, 1, True), g1, w_bf, wt_bf, tm=512, transposed=True)
    oa_t = _attn_prompt(qt, kb, vt, g_col, lams, tq=1024, tk=512)

    tm_r = 256
    prep_p = _rwkv_prep(zr_p, None, tm_r, tm_r, rw)
    or_p, s_p = _rwkv_scan(prep_p, lnw, lnb, avg, jnp.zeros((H_R // 2, LANES, LANES), F32),
                           chunk=64, carry=True)
    mods_p = (mod_rows(0, 2, True), mod_rows(1, 0, True), mod_rows(1, 1, True), mod_rows(1, 2, True))
    y_p = _ffn(xp, oa_t, or_p, mods_p, g2, gf, wo_bf, wr, br, wg_bf, wu_bf, wd_bf,
               tm=1024, oa_transposed=True)

    xs = x_sample.reshape(n_s, D_MODEL)
    q_s, _, _, k_s, v_s, zr_s = _in_proj(
        xs, mod_rows(0, 0, False), mod_rows(0, 1, False), g1, w_bf, wt_bf, tm=256, transposed=False)
    q5 = q_s.reshape(bs, ts, 2 * H_A, DK_A)
    eye_hm = jnp.eye(2 * H_A, dtype=BF16)
    qbd = q5.transpose(0, 2, 1, 3)[:, :, :, None, :] * eye_hm[None, :, None, :, None]
    qbd = qbd.reshape(bs, 2 * H_A * ts, C_QA)
    n_pool = cache_k.shape[1]
    kt_pages = cache_k[l].transpose(0, 2, 3, 4, 1).reshape(n_pool, C_QA, PAGE_SIZE)
    vt_pages = cache_v[l].transpose(0, 2, 3, 1).reshape(n_pool, W_A, PAGE_SIZE)
    oa_s = _attn_sample(page_table, qbd, k_s.reshape(bs, ts, C_QA), v_s.reshape(bs, ts, W_A),
                        kt_pages, vt_pages, g_row, seg, lams)

    first_s = jnp.zeros((bs, ts, C_R), F32).at[:, 0].set(state_shift[l]).reshape(n_s, C_R)
    prep_s = _rwkv_prep(zr_s, first_s, ts, 256, rw)
    or_s, s_s = _rwkv_scan(prep_s, lnw, lnb, avg, _block_diag_state(state_wkv[l]),
                           chunk=ts, carry=False)
    mods_s = (mod_rows(0, 2, False), mod_rows(1, 0, False), mod_rows(1, 1, False), mod_rows(1, 2, False))
    y_s = _ffn(xs, oa_s.reshape(n_s, W_A), or_s, mods_s, g2, gf, wo_bf, wr, br, wg_bf, wu_bf, wd_bf,
               tm=512, oa_transposed=False)

    return (
        y_p.reshape(bp, sp, D_MODEL),
        y_s.reshape(bs, ts, D_MODEL),
        k_p.reshape(1, bp, sp, H_A, 2, DK_A),
        v_p.reshape(1, bp, sp, H_A, DV_A),
        _diag_state(s_p[None]).astype(state_wkv.dtype)[None],
        zr_p[n_p - 1:].reshape(1, bp, C_R),
        k_s.reshape(1, bs, ts, H_A, 2, DK_A),
        v_s.reshape(1, bs, ts, H_A, DV_A),
        _diag_state(s_s).astype(state_wkv.dtype)[None],
        zr_s.reshape(bs, ts, C_R)[:, -1][None],
    )
```

```python
import functools
import math

import jax
import jax.numpy as jnp
from jax import lax
from jax.experimental import pallas as pl
from jax.experimental.pallas import tpu as pltpu

F32 = jnp.float32
BF16 = jnp.bfloat16

D_MODEL = 1024
PAGE_SIZE = 128
H_A = 8
DK_A = 32
DV_A = 64
H_R = 8
HD_R = 64
D_DECAY_LORA = 64
D_AAA_LORA = 64
D_GATE_LORA = 128
N_GROUPS = 4
EXPERTS_PER_GROUP = 4
N_EXPERTS = N_GROUPS * EXPERTS_PER_GROUP
D_EXPERT = 512
NORM_EPS = 1e-6
LN_X_EPS = 64e-5
NEG_INF = -1e30
W_A = H_A * DV_A
W_R = H_R * HD_R
C_QA = H_A * 2 * DK_A
C_A = 2 * C_QA + W_A
C_R = 3 * W_R + D_DECAY_LORA + D_AAA_LORA + D_GATE_LORA
LAM_INIT = 0.8 - 0.6 * math.exp(-0.3 * 0)

LANES = 128
LOG2E = 1.4426950408889634
Q_SCALE = DK_A ** -0.5 * LOG2E
VMEM_LIMIT = 56 * 1024 * 1024


def _dot(a, b):
    return jnp.dot(a, b, preferred_element_type=F32)


def _dot_nt(a, b):
    return lax.dot_general(a, b, (((1,), (1,)), ((), ())), preferred_element_type=F32)


def _dot_tn(a, b):
    return lax.dot_general(a, b, (((0,), (0,)), ((), ())), preferred_element_type=F32)


def _split3(x):
    hi = x.astype(BF16)
    r1 = x - hi.astype(F32)
    mid = r1.astype(BF16)
    lo = (r1 - mid.astype(F32)).astype(BF16)
    return hi, mid, lo


def _dot_exact_rhs(x, m_bf16):
    hi, mid, lo = _split3(x)
    return _dot(hi, m_bf16) + _dot(mid, m_bf16) + _dot(lo, m_bf16)


def _dot_exact_lhs(m_bf16, x):
    hi, mid, lo = _split3(x)
    return _dot(m_bf16, hi) + _dot(m_bf16, mid) + _dot(m_bf16, lo)


def _dot_f32(a, b):
    ah = a.astype(BF16)
    al = (a - ah.astype(F32)).astype(BF16)
    bh = b.astype(BF16)
    bl = (b - bh.astype(F32)).astype(BF16)
    return _dot(ah, bh) + _dot(ah, bl) + _dot(al, bh)


def _sigmoid(x):
    return 1.0 / (1.0 + jnp.exp(-x))


def _params(*sem):
    return pltpu.CompilerParams(dimension_semantics=sem, vmem_limit_bytes=VMEM_LIMIT)


def _ada_kernel(c_ref, w_ref, b_ref, o_ref):
    c = c_ref[...]
    s = (c * _sigmoid(c)).astype(BF16)
    o_ref[0] = _dot(s, w_ref[0]) + b_ref[0]


def _ada_mod(c_all, ada_w, ada_b):
    r = c_all.shape[0]
    n_col = 3
    return pl.pallas_call(
        _ada_kernel,
        grid=(2, n_col),
        in_specs=[
            pl.BlockSpec((r, D_MODEL), lambda s, j: (0, 0)),
            pl.BlockSpec((1, D_MODEL, D_MODEL), lambda s, j: (s, 0, j)),
            pl.BlockSpec((1, 1, D_MODEL), lambda s, j: (s, 0, j)),
        ],
        out_specs=pl.BlockSpec((1, r, D_MODEL), lambda s, j: (s, 0, j)),
        out_shape=jax.ShapeDtypeStruct((2, r, 3 * D_MODEL), F32),
        compiler_params=_params("arbitrary", "arbitrary"),
        name="ada_mod",
    )(c_all, ada_w, ada_b.reshape(2, 1, 3 * D_MODEL))


def _modulated_norm(x, g, sc, sh):
    ms = jnp.mean(x * x, axis=-1, keepdims=True)
    return (x * lax.rsqrt(ms + NORM_EPS) * g) * (1.0 + sc) + sh


def _inproj_kernel(x_ref, sh_ref, sc_ref, g_ref, w_ref, wt_ref,
                   q_ref, kb_ref, vb_ref, k_ref, v_ref, zr_ref, *, transposed):
    hb = _modulated_norm(x_ref[...], g_ref[...], sc_ref[...], sh_ref[...]).astype(BF16)
    zk = _dot(hb, w_ref[:, C_QA:2 * C_QA])
    k_ref[...] = zk
    kb_ref[...] = zk.astype(BF16)
    zv = _dot(hb, w_ref[:, 2 * C_QA:C_A])
    v_ref[...] = zv
    zr_ref[...] = _dot(hb, w_ref[:, C_A:])
    if transposed:
        q_ref[...] = (_dot_nt(wt_ref[0:C_QA], hb) * Q_SCALE).astype(BF16)
        vb_ref[...] = _dot_nt(wt_ref[C_QA:C_QA + W_A], hb).astype(BF16)
    else:
        q_ref[...] = (_dot(hb, w_ref[:, 0:C_QA]) * Q_SCALE).astype(BF16)
        vb_ref[...] = zv.astype(BF16)


def _in_proj(x, sh, sc, g, w_bf, wt_bf, *, tm, transposed):
    n = x.shape[0]
    per_tok = sh.shape[0] != 1
    mod_spec = (pl.BlockSpec((tm, D_MODEL), lambda i: (i, 0)) if per_tok
                else pl.BlockSpec((1, D_MODEL), lambda i: (0, 0)))
    tok = lambda w: pl.BlockSpec((tm, w), lambda i: (i, 0))
    if transposed:
        qv_spec = pl.BlockSpec((C_QA, tm), lambda i: (0, i))
        qv_shape = jax.ShapeDtypeStruct((C_QA, n), BF16)
    else:
        qv_spec = tok(C_QA)
        qv_shape = jax.ShapeDtypeStruct((n, C_QA), BF16)
    return pl.pallas_call(
        functools.partial(_inproj_kernel, transposed=transposed),
        grid=(n // tm,),
        in_specs=[
            tok(D_MODEL), mod_spec, mod_spec,
            pl.BlockSpec((1, D_MODEL), lambda i: (0, 0)),
            pl.BlockSpec(w_bf.shape, lambda i: (0, 0)),
            pl.BlockSpec(wt_bf.shape, lambda i: (0, 0)),
        ],
        out_specs=[qv_spec, tok(C_QA), qv_spec, tok(C_QA), tok(W_A), tok(C_R)],
        out_shape=[
            qv_shape,
            jax.ShapeDtypeStruct((n, C_QA), BF16),
            qv_shape,
            jax.ShapeDtypeStruct((n, C_QA), F32),
            jax.ShapeDtypeStruct((n, W_A), F32),
            jax.ShapeDtypeStruct((n, C_R), F32),
        ],
        compiler_params=_params("arbitrary"),
        name="in_proj_t" if transposed else "in_proj",
    )(x, sh, sc, g, w_bf, wt_bf)


def _lambda(q1_ref, k1_ref, q2_ref, k2_ref):
    l1 = jnp.sum(q1_ref[...] * k1_ref[...], axis=-1, keepdims=True)
    l2 = jnp.sum(q2_ref[...] * k2_ref[...], axis=-1, keepdims=True)
    return jnp.exp(l1) - jnp.exp(l2) + LAM_INIT


ACC_ROWS = LANES + 16

def _attn_prompt_kernel(qt_ref, k_ref, vt_ref, g_ref, lq1, lk1, lq2, lk2, o_ref,
                        m_scr, acc_scr, *, tq, tk):
    i = pl.program_id(1)
    n_sub = tq // tk
    m_scr[...] = jnp.full(m_scr.shape, NEG_INF, F32)
    acc_scr[...] = jnp.zeros(acc_scr.shape, F32)

    qt = qt_ref[...]
    row_q = lax.broadcasted_iota(jnp.int32, (LANES, tq), 0)
    qtm = [jnp.where((row_q >= DK_A * i4) & (row_q < DK_A * (i4 + 1)), qt, jnp.zeros_like(qt))
           for i4 in range(4)]
    row_v = lax.broadcasted_iota(jnp.int32, (LANES, tk), 0)
    first_head = row_q < DV_A
    row_a = lax.broadcasted_iota(jnp.int32, (ACC_ROWS, tq), 0)
    first_head_acc = (row_a < DV_A) | (row_a == LANES)
    row_1 = lax.broadcasted_iota(jnp.int32, (ACC_ROWS - LANES, 2 * tk), 0)
    col_1 = lax.broadcasted_iota(jnp.int32, (ACC_ROWS - LANES, 2 * tk), 1)
    ones_rows = jnp.where(((row_1 == 0) & (col_1 < tk)) | ((row_1 == 1) & (col_1 >= tk)), 1.0, 0.0).astype(BF16)

    def step(j, diag_offset):
        start = pl.multiple_of(j * tk, tk)
        kblk = k_ref[pl.ds(start, tk), :]
        vtblk = vt_ref[:, pl.ds(start, tk)]
        if diag_offset is not None:
            visible = (lax.broadcasted_iota(jnp.int32, (tk, tq), 0) + diag_offset
                       <= lax.broadcasted_iota(jnp.int32, (tk, tq), 1))
        pts, alphas = [], []
        for i4 in range(4):
            st = _dot(kblk, qtm[i4])
            if diag_offset is not None:
                st = jnp.where(visible, st, NEG_INF)
            m_prev = m_scr[i4]
            m_new = jnp.maximum(m_prev, jnp.max(st, axis=0, keepdims=True))
            alphas.append(jnp.exp2(m_prev - m_new))
            pts.append(jnp.exp2((st - m_new).astype(BF16)))
            m_scr[i4] = m_new
        vbd = jnp.concatenate(
            [jnp.concatenate(
                [jnp.where(row_v < DV_A, vtblk, jnp.zeros_like(vtblk)),
                 jnp.where(row_v >= DV_A, vtblk, jnp.zeros_like(vtblk))], axis=1),
             ones_rows], axis=0)
        for m in range(2):
            pcat = jnp.concatenate([pts[m], pts[2 + m]], axis=0)
            alpha_m = jnp.where(first_head_acc, alphas[m], alphas[2 + m])
            acc_scr[m] = acc_scr[m] * alpha_m + _dot(vbd, pcat)

    def body(jj, carry):
        for u in range(n_sub):
            step(jj * n_sub + u, None)
        return carry

    lax.fori_loop(0, i, body, 0)
    for u in range(n_sub):
        step(i * n_sub + u, u * tk)

    lam = _lambda(lq1, lk1, lq2, lk2)
    outs = []
    for m in range(2):
        acc = acc_scr[m]
        den = jnp.where(first_head, acc[LANES:LANES + 1], acc[LANES + 1:LANES + 2])
        outs.append(acc[:LANES] / den)
    o = (outs[0] - lam * outs[1]).reshape(2, DV_A, tq)
    ms = jnp.mean(o * o, axis=1, keepdims=True)
    o = o * lax.rsqrt(ms + NORM_EPS)
    o = o.reshape(LANES, tq) * g_ref[:, 0:1] * (1.0 - LAM_INIT)
    o_ref[...] = o.astype(o_ref.dtype)


def _attn_prompt(qt, k_bf, vt, g_col, lams, *, tq, tk):
    s = qt.shape[1]
    lam_spec = pl.BlockSpec((1, DK_A), lambda p, i: (0, 0))
    return pl.pallas_call(
        functools.partial(_attn_prompt_kernel, tq=tq, tk=tk),
        grid=(H_A // 2, s // tq),
        in_specs=[
            pl.BlockSpec((LANES, tq), lambda p, i: (p, i)),
            pl.BlockSpec((s, LANES), lambda p, i: (0, p)),
            pl.BlockSpec((LANES, s), lambda p, i: (p, 0)),
            pl.BlockSpec((LANES, LANES), lambda p, i: (0, 0)),
            lam_spec, lam_spec, lam_spec, lam_spec,
        ],
        out_specs=pl.BlockSpec((LANES, tq), lambda p, i: (p, i)),
        out_shape=jax.ShapeDtypeStruct((W_A, s), BF16),
        scratch_shapes=[
            pltpu.VMEM((4, 1, tq), F32),
            pltpu.VMEM((2, ACC_ROWS, tq), F32),
        ],
        compiler_params=_params("arbitrary", "arbitrary"),
        name="attn_prompt",
    )(qt, k_bf, vt, g_col, *lams)


PAGES_PER_STEP = 16
SEQS_PER_STEP = 2


def _attn_sample_kernel(pt_ref, qbd_ref, kn_ref, vn_ref, g_ref, seg_ref, lq1, lk1, lq2, lk2, *rest,
                        n_steps, t_new):
    n_pg = SEQS_PER_STEP * PAGES_PER_STEP
    k_refs, v_refs = rest[:n_pg], rest[n_pg:2 * n_pg]
    o_ref, m_scr, l_scr, acc_scr = rest[2 * n_pg:]
    c = pl.program_id(1)

    @pl.when(c == 0)
    def _():
        m_scr[...] = jnp.full(m_scr.shape, NEG_INF, F32)
        l_scr[...] = jnp.zeros(l_scr.shape, F32)
        acc_scr[...] = jnp.zeros(acc_scr.shape, F32)

    def update(q, s, pv):
        m_prev = m_scr[q]
        m_new = jnp.maximum(m_prev, jnp.max(s, axis=1, keepdims=True))
        alpha = jnp.exp2(m_prev - m_new)
        p = jnp.exp2(s - m_new)
        l_scr[q] = alpha * l_scr[q] + jnp.sum(p, axis=1, keepdims=True)
        m_scr[q] = m_new
        acc_scr[q] = acc_scr[q] * alpha + pv(p.astype(BF16))

    for q in range(SEQS_PER_STEP):
        pages = slice(q * PAGES_PER_STEP, (q + 1) * PAGES_PER_STEP)
        kt = jnp.concatenate([r[0] for r in k_refs[pages]], axis=1).astype(BF16)
        vt = jnp.concatenate([r[0] for r in v_refs[pages]], axis=1).astype(BF16)
        update(q, _dot(qbd_ref[q], kt), lambda p, vt=vt: _dot_nt(p, vt))

    @pl.when(c == n_steps - 1)
    def _():
        lam = _lambda(lq1, lk1, lq2, lk2)
        pad = jnp.zeros((16 - t_new, C_QA), F32)
        trow = lax.broadcasted_iota(jnp.int32, (LANES, 16), 0) % t_new
        jcol = lax.broadcasted_iota(jnp.int32, (LANES, 16), 1)
        lane_head = lax.broadcasted_iota(jnp.int32, (t_new, W_A), 1) // DV_A
        for q in range(SEQS_PER_STEP):
            kn = jnp.concatenate([kn_ref[q], pad], axis=0).astype(BF16)
            vn = jnp.concatenate([vn_ref[q], pad], axis=0).astype(BF16)
            s_new = jnp.where(jcol <= trow, _dot_nt(qbd_ref[q], kn), NEG_INF)
            update(q, s_new, lambda p, vn=vn: _dot(p, vn))

            accn = acc_scr[q] / l_scr[q]
            outs = []
            for m in range(2):
                o_m = jnp.zeros((t_new, W_A), F32)
                for h in range(H_A):
                    r0 = h * 2 * t_new + m * t_new
                    o_m = jnp.where(lane_head == h, accn[r0:r0 + t_new, :], o_m)
                outs.append(o_m)
            o = outs[0] - lam * outs[1]
            ms = _dot_exact_rhs(o * o, seg_ref[...]) * (1.0 / DV_A)
            o_ref[q] = o * lax.rsqrt(ms + NORM_EPS) * g_ref[...] * (1.0 - LAM_INIT)


def _attn_sample(page_table, qbd, k_new, v_new, kt_pages, vt_pages, g_row, seg, lams):
    b, t_new = k_new.shape[0], k_new.shape[1]
    n_pages = page_table.shape[1]
    n_steps = n_pages // PAGES_PER_STEP
    nq = SEQS_PER_STEP
    pt_flat = page_table.reshape(-1)

    def page_spec(q, j):
        return pl.BlockSpec(
            (1, C_QA, PAGE_SIZE),
            lambda bi, c, pt: (pt[(bi * nq + q) * n_pages + c * PAGES_PER_STEP + j], 0, 0))

    page_specs = [page_spec(q, j) for q in range(nq) for j in range(PAGES_PER_STEP)]
    lam_spec = pl.BlockSpec((1, DK_A), lambda bi, c, pt: (0, 0))
    seq = lambda shp: pl.BlockSpec(shp, lambda bi, c, pt: (bi, 0, 0))
    grid_spec = pltpu.PrefetchScalarGridSpec(
        num_scalar_prefetch=1,
        grid=(b // nq, n_steps),
        in_specs=[
            seq((nq, LANES, C_QA)), seq((nq, t_new, C_QA)), seq((nq, t_new, W_A)),
            pl.BlockSpec((1, W_A), lambda bi, c, pt: (0, 0)),
            pl.BlockSpec((W_A, W_A), lambda bi, c, pt: (0, 0)),
            lam_spec, lam_spec, lam_spec, lam_spec,
        ] + page_specs * 2,
        out_specs=seq((nq, t_new, W_A)),
        scratch_shapes=[
            pltpu.VMEM((nq, LANES, 1), F32),
            pltpu.VMEM((nq, LANES, 1), F32),
            pltpu.VMEM((nq, LANES, W_A), F32),
        ],
    )
    return pl.pallas_call(
        functools.partial(_attn_sample_kernel, n_steps=n_steps, t_new=t_new),
        grid_spec=grid_spec,
        out_shape=jax.ShapeDtypeStruct((b, t_new, W_A), F32),
        compiler_params=_params("arbitrary", "arbitrary"),
        name="attn_sample",
    )(pt_flat, qbd, k_new, v_new, g_row, seg, *lams,
      *([kt_pages] * (nq * PAGES_PER_STEP)), *([vt_pages] * (nq * PAGES_PER_STEP)))


def _rwkv_prep_kernel(zr_ref, first_ref, mu_ref, w0_ref, w2_ref, a0_ref, a2_ref, g2_ref,
                      kk_ref, ka_ref, rk_ref, seg_ref,
                      r_o, lw_o, k_o, v_o, kn_o, b_o, g_o, bonus_o, *, period):
    z = zr_ref[...]
    row = lax.broadcasted_iota(jnp.int32, z.shape, 0)
    if first_ref.shape[0] == z.shape[0]:
        first = first_ref[...]
    else:
        first = jnp.where(pl.program_id(0) == 0, 0.0, first_ref[7:8, :])
    z_prev = jnp.where((row & (period - 1)) == 0, first, pltpu.roll(z, 1, 0))
    z = z + (z_prev - z) * mu_ref[...]
    o1, o2, o3 = W_R, 2 * W_R, 3 * W_R
    o5 = o3 + D_DECAY_LORA + D_AAA_LORA
    r, k, v = z[:, :o1], z[:, o1:o2], z[:, o2:o3]
    zwa = z[:, o3:o5]
    zg = z[:, o5:]
    w_pre = w0_ref[...] + _dot_f32(jnp.tanh(zwa), w2_ref[...])
    w_raw = -(jnp.maximum(-w_pre, 0.0) + jnp.log(1.0 + jnp.exp(-jnp.abs(w_pre)))) - 0.5
    lw_o[...] = -jnp.exp(w_raw)
    a = _sigmoid(a0_ref[...] + _dot(zwa.astype(BF16), a2_ref[...]))
    g_o[...] = _dot(_sigmoid(zg).astype(BF16), g2_ref[...])
    kk = k * kk_ref[...]
    ssq = _dot_exact_rhs(kk * kk, seg_ref[...])
    kn = kk * lax.rsqrt(ssq + 1e-12)
    k2 = k * (1.0 + (a - 1.0) * ka_ref[...])
    rk = _dot_exact_rhs(r * k2 * rk_ref[...], seg_ref[...])
    r_o[...] = r
    k_o[...] = k2
    v_o[...] = v
    kn_o[...] = kn
    b_o[...] = kn * a
    bonus_o[...] = rk * v


def _rwkv_prep(zr, first, period, tm, p):
    n = zr.shape[0]
    assert period & (period - 1) == 0
    tok = lambda w: pl.BlockSpec((tm, w), lambda i: (i, 0))
    full = lambda a: pl.BlockSpec(a.shape, lambda i: (0,) * a.ndim)
    if first is None:
        first, first_spec = zr, pl.BlockSpec((8, C_R), lambda i: (jnp.maximum(i * (tm // 8) - 1, 0), 0))
    else:
        first_spec = tok(C_R)
    consts = [p["mu"], p["w0"], p["w2"], p["a0"], p["a2"], p["g2"], p["kk"], p["ka"], p["rk"], p["seg"]]
    return pl.pallas_call(
        functools.partial(_rwkv_prep_kernel, period=period),
        grid=(n // tm,),
        in_specs=[tok(C_R), first_spec] + [full(a) for a in consts],
        out_specs=[tok(W_R)] * 8,
        out_shape=[jax.ShapeDtypeStruct((n, W_R), F32)] * 8,
        compiler_params=_params("arbitrary"),
        name="rwkv_prep",
    )(zr, first, *consts)


SUPER = 256


def _rwkv_scan_kernel(r_ref, lw_ref, k_ref, v_ref, kn_ref, b_ref, g_ref, bonus_ref,
                      lnw_ref, lnb_ref, avg_ref, s_in_ref, o_ref, s_out_ref, *, chunk, carry):
    n_blk = SUPER // chunk
    n_pair = H_R // 2
    shift = chunk.bit_length() - 1
    ri = lax.broadcasted_iota(jnp.int32, (SUPER, SUPER), 0)
    ci = lax.broadcasted_iota(jnp.int32, (SUPER, SUPER), 1)
    same = (ri >> shift) == (ci >> shift)
    strict = same & (ci < ri)
    incl = same & (ci <= ri)
    m_incl = jnp.where(incl, 1.0, 0.0).astype(BF16)

    if carry:
        @pl.when(pl.program_id(0) == 0)
        def _():
            s_out_ref[...] = s_in_ref[...]

    lw = lw_ref[...]
    cum = _dot_exact_lhs(m_incl, lw)
    gi = jnp.exp(cum)
    a_t = -kn_ref[...] * jnp.exp(cum - lw)
    ginv = jnp.exp(-cum)
    b_t = b_ref[...] * ginv
    k_t = k_ref[...] * ginv
    r_t = r_ref[...] * gi
    v_all = v_ref[...]

    lane = lax.broadcasted_iota(jnp.int32, (SUPER, LANES), 1)
    bd = ((lax.broadcasted_iota(jnp.int32, (LANES, LANES), 0) // HD_R)
          == (lax.broadcasted_iota(jnp.int32, (LANES, LANES), 1) // HD_R))
    col_blk = lax.broadcasted_iota(jnp.int32, (LANES, SUPER), 1) >> shift

    pre = []
    for p in range(n_pair):
        ls = slice(p * LANES, (p + 1) * LANES)
        a_p, b_p, k_p, r_p, v_p = a_t[:, ls], b_t[:, ls], k_t[:, ls], r_t[:, ls], v_all[:, ls]
        b_pb, k_pb = b_p.astype(BF16), k_p.astype(BF16)
        a2_p = jnp.zeros((SUPER, LANES), F32)
        p0_p = jnp.zeros((SUPER, LANES), F32)
        y0_p = jnp.zeros((SUPER, LANES), F32)
        wrbs = []
        for hh in range(2):
            hm = (lane // HD_R) == hh
            a_m = jnp.where(hm, a_p, 0.0)
            a_mb = a_m.astype(BF16)
            r_mb = jnp.where(hm, r_p, 0.0).astype(BF16)
            v_mb = jnp.where(hm, v_p, 0.0).astype(BF16)
            l_ab = jnp.where(strict, _dot_nt(a_mb, b_pb), 0.0)
            l_ak = jnp.where(strict, _dot_nt(a_mb, k_pb), 0.0)
            w_rb = jnp.where(incl, _dot_nt(r_mb, b_pb), 0.0)
            w_rk = jnp.where(incl, _dot_nt(r_mb, k_pb), 0.0)
            t_m = l_ab
            l_pow = l_ab
            for _ in range(shift - 1):
                lb = l_pow.astype(BF16)
                l_pow = _dot(lb, lb)
                t_m = t_m + l_pow + _dot(t_m.astype(BF16), l_pow.astype(BF16))
            t_mb = t_m.astype(BF16)
            a2_p = a2_p + a_m + _dot(t_mb, a_mb)
            u0 = _dot(l_ak.astype(BF16), v_mb)
            p0_p = p0_p + u0 + _dot(t_mb, u0.astype(BF16))
            y0_p = y0_p + _dot(w_rk.astype(BF16), v_mb)
            wrbs.append(w_rb.astype(BF16))
        pre.append(dict(
            ar=jnp.concatenate([a2_p, r_p], axis=0).astype(BF16),
            bk=jnp.concatenate([b_pb, k_pb], axis=0),
            p0t=jnp.transpose(p0_p),
            vt=jnp.transpose(v_p).astype(BF16),
            y0=y0_p, wrbs=wrbs))

    states = [s_out_ref[p] for p in range(n_pair)] if carry else [None] * n_pair
    pt_acc = [jnp.zeros((LANES, SUPER), F32) for _ in range(n_pair)]
    yrt_acc = [jnp.zeros((LANES, SUPER), F32) for _ in range(n_pair)]
    for c in range(n_blk):
        in_blk = col_blk == c
        for p in range(n_pair):
            q = pre[p]
            s = states[p] if carry else s_in_ref[c, p]
            out = _dot_nt(s.astype(BF16), q["ar"])
            ptm = jnp.where(in_blk, out[:, :SUPER] + q["p0t"], 0.0)
            pt_acc[p] = pt_acc[p] + ptm
            yrt_acc[p] = jnp.where(in_blk, out[:, SUPER:], yrt_acc[p])
            lhs = jnp.concatenate([ptm.astype(BF16), jnp.where(in_blk, q["vt"], jnp.zeros_like(q["vt"]))], axis=1)
            upd = _dot(lhs, q["bk"])
            g_end = gi[(c + 1) * chunk - 1:(c + 1) * chunk, p * LANES:(p + 1) * LANES]
            s = (s + jnp.where(bd, upd, 0.0)) * g_end
            if carry:
                states[p] = s
            else:
                s_out_ref[c, p] = s

    for p in range(n_pair):
        ls = slice(p * LANES, (p + 1) * LANES)
        q = pre[p]
        if carry:
            s_out_ref[p] = states[p]
        p_full = jnp.transpose(pt_acc[p])
        y = jnp.transpose(yrt_acc[p]) + q["y0"]
        for hh in range(2):
            hm = (lane // HD_R) == hh
            y = y + _dot(q["wrbs"][hh], jnp.where(hm, p_full, 0.0).astype(BF16))
        mean = _dot_exact_rhs(y, avg_ref[...])
        yc = y - mean
        var = _dot_exact_rhs(yc * yc, avg_ref[...])
        yn = yc * lax.rsqrt(var + LN_X_EPS) * lnw_ref[:, ls] + lnb_ref[:, ls]
        o_ref[:, ls] = ((yn + bonus_ref[:, ls]) * g_ref[:, ls]).astype(o_ref.dtype)


def _rwkv_scan(prep, lnw, lnb, avg, s_in, *, chunk, carry):
    n = prep[0].shape[0]
    n_blk = SUPER // chunk
    tok = pl.BlockSpec((SUPER, W_R), lambda i: (i, 0))
    row = pl.BlockSpec((1, W_R), lambda i: (0, 0))
    if carry:
        s_spec = pl.BlockSpec(s_in.shape, lambda i: (0, 0, 0))
    else:
        s_spec = pl.BlockSpec((n_blk,) + s_in.shape[1:], lambda i: (i, 0, 0, 0))
    return pl.pallas_call(
        functools.partial(_rwkv_scan_kernel, chunk=chunk, carry=carry),
        grid=(n // SUPER,),
        in_specs=[tok] * 8 + [row, row, pl.BlockSpec((LANES, LANES), lambda i: (0, 0)), s_spec],
        out_specs=[tok, s_spec],
        out_shape=[jax.ShapeDtypeStruct((n, W_R), BF16), jax.ShapeDtypeStruct(s_in.shape, F32)],
        compiler_params=_params("arbitrary"),
        name="rwkv_scan_carry" if carry else "rwkv_scan_seq",
    )(*prep, lnw, lnb, avg, s_in)


ROUTE_W = 128
EXPERT_LANE0 = 8


def _route(logits):
    lane = lax.broadcasted_iota(jnp.int32, logits.shape, 1)
    big = jnp.int32(1 << 20)
    g_log = jnp.where(lane < N_GROUPS, logits, -jnp.inf)
    g_max = jnp.max(g_log, axis=-1, keepdims=True)
    g_idx = jnp.min(jnp.where(g_log == g_max, lane, big), axis=-1, keepdims=True)
    g_w = 1.0 / jnp.sum(jnp.exp(g_log - g_max), axis=-1, keepdims=True)
    e_lo = EXPERT_LANE0 + g_idx * EXPERTS_PER_GROUP
    e_log = jnp.where((lane >= e_lo) & (lane < e_lo + EXPERTS_PER_GROUP), logits, -jnp.inf)
    e_max = jnp.max(e_log, axis=-1, keepdims=True)
    i1 = jnp.min(jnp.where(e_log == e_max, lane, big), axis=-1, keepdims=True)
    e_log2 = jnp.where(lane == i1, -jnp.inf, e_log)
    e_max2 = jnp.max(e_log2, axis=-1, keepdims=True)
    i2 = jnp.min(jnp.where(e_log2 == e_max2, lane, big), axis=-1, keepdims=True)
    p2 = jnp.exp(e_max2 - e_max)
    w1 = g_w / (1.0 + p2)
    w2 = g_w * p2 / (1.0 + p2)
    return jnp.where(lane == i1, w1, 0.0) + jnp.where(lane == i2, w2, 0.0)


def _ffn_kernel(x_ref, oa_ref, or_ref, gt1_ref, sh_ref, sc_ref, gt2_ref, g2_ref, gf_ref,
                wo_ref, wr_ref, br_ref, wg_ref, wu_ref, wd_ref, y_ref,
                x1_scr, h_scr, comb_scr, acc_scr, *, oa_transposed):
    e = pl.program_id(1)

    @pl.when(e == 0)
    def _():
        if oa_transposed:
            mix = _dot_tn(oa_ref[...], wo_ref[0:W_A])
        else:
            mix = _dot(oa_ref[...].astype(BF16), wo_ref[0:W_A])
        mix = mix + _dot(or_ref[...], wo_ref[W_A:])
        x1 = x_ref[...] + gt1_ref[...] * mix
        x1_scr[...] = x1
        h = _modulated_norm(x1, g2_ref[...], sc_ref[...], sh_ref[...])
        h_scr[...] = h.astype(BF16)
        comb_scr[...] = _route(_dot_f32(h, wr_ref[...]) + br_ref[...])
        acc_scr[...] = jnp.zeros(acc_scr.shape, F32)

    hb = h_scr[...]
    hg = _dot(hb, wg_ref[0])
    hu = _dot(hb, wu_ref[0])
    lane = lax.broadcasted_iota(jnp.int32, comb_scr.shape, 1)
    comb_e = jnp.sum(jnp.where(lane == EXPERT_LANE0 + e, comb_scr[...], 0.0), axis=-1, keepdims=True)
    act = (hg * _sigmoid(hg)) * hu * comb_e
    acc_scr[...] += _dot(act.astype(BF16), wd_ref[0])

    @pl.when(e == N_EXPERTS - 1)
    def _():
        x2 = x1_scr[...] + gt2_ref[...] * acc_scr[...]
        ms = jnp.mean(x2 * x2, axis=-1, keepdims=True)
        y_ref[...] = x2 * lax.rsqrt(ms + NORM_EPS) * gf_ref[...]


def _ffn(x, oa, o_r, mods, g2, gf, wo_bf, wr, br, wg_bf, wu_bf, wd_bf, *, tm, oa_transposed):
    n = x.shape[0]
    per_tok = mods[0].shape[0] != 1
    mod_spec = (pl.BlockSpec((tm, D_MODEL), lambda i, e: (i, 0)) if per_tok
                else pl.BlockSpec((1, D_MODEL), lambda i, e: (0, 0)))
    tok = lambda w: pl.BlockSpec((tm, w), lambda i, e: (i, 0))
    oa_spec = pl.BlockSpec((W_A, tm), lambda i, e: (0, i)) if oa_transposed else tok(W_A)
    row = pl.BlockSpec((1, D_MODEL), lambda i, e: (0, 0))
    return pl.pallas_call(
        functools.partial(_ffn_kernel, oa_transposed=oa_transposed),
        grid=(n // tm, N_EXPERTS),
        in_specs=[
            tok(D_MODEL), oa_spec, tok(W_R), mod_spec, mod_spec, mod_spec, mod_spec, row, row,
            pl.BlockSpec(wo_bf.shape, lambda i, e: (0, 0)),
            pl.BlockSpec(wr.shape, lambda i, e: (0, 0)),
            pl.BlockSpec((1, ROUTE_W), lambda i, e: (0, 0)),
            pl.BlockSpec((1, D_MODEL, D_EXPERT), lambda i, e: (e, 0, 0)),
            pl.BlockSpec((1, D_MODEL, D_EXPERT), lambda i, e: (e, 0, 0)),
            pl.BlockSpec((1, D_EXPERT, D_MODEL), lambda i, e: (e, 0, 0)),
        ],
        out_specs=tok(D_MODEL),
        out_shape=jax.ShapeDtypeStruct((n, D_MODEL), F32),
        scratch_shapes=[
            pltpu.VMEM((tm, D_MODEL), F32),
            pltpu.VMEM((tm, D_MODEL), BF16),
            pltpu.VMEM((tm, ROUTE_W), F32),
            pltpu.VMEM((tm, D_MODEL), F32),
        ],
        compiler_params=_params("arbitrary", "arbitrary"),
        name="ffn_t" if oa_transposed else "ffn",
    )(x, oa, o_r, *mods, g2, gf, wo_bf, wr, br, wg_bf, wu_bf, wd_bf)


def _block_diag_state(s):
    b = s.shape[0]
    s = s.reshape(b, H_R // 2, 2, HD_R, HD_R)
    z = jnp.zeros_like(s[:, :, 0])
    top = jnp.concatenate([s[:, :, 0], z], axis=-1)
    bot = jnp.concatenate([z, s[:, :, 1]], axis=-1)
    return jnp.concatenate([top, bot], axis=-2)


def _diag_state(sbd):
    b = sbd.shape[0]
    h0 = sbd[:, :, :HD_R, :HD_R]
    h1 = sbd[:, :, HD_R:, HD_R:]
    return jnp.stack([h0, h1], axis=2).reshape(b, H_R, HD_R, HD_R)


def kernel(x_prompt, x_sample, c_prompt, c_sample, cache_k, cache_v, state_wkv, state_shift, page_table, ada_w, ada_b, norm_mix_g, norm_ffn_g, w_in, w_out, lam_q1, lam_k1, lam_q2, lam_k2, subln_g, rw_mu, rw_w0, rw_w2, rw_a0, rw_a2, rw_g2, rw_kk, rw_ka, rw_rk, rw_lnw, rw_lnb, moe_wg, moe_bg, moe_we, moe_be, moe_w_gate, moe_w_up, moe_w_down, final_g):
    bp, sp, _ = x_prompt.shape
    bs, ts, _ = x_sample.shape
    assert bp == 1 and ada_w.shape[0] == 1, "single prompt row and single layer only"
    n_p, n_s = bp * sp, bs * ts
    l = 0

    w_bf = w_in[l].astype(BF16)
    wt_bf = jnp.concatenate([w_in[l][:, :C_QA].T, w_in[l][:, 2 * C_QA:C_A].T], axis=0).astype(BF16)
    wo_bf = w_out[l].astype(BF16)
    lams = [a[l].reshape(1, DK_A) for a in (lam_q1, lam_k1, lam_q2, lam_k2)]
    g_col = jnp.broadcast_to(jnp.tile(subln_g[l], 2)[:, None], (LANES, LANES))
    g_row = jnp.tile(subln_g[l], H_A).reshape(1, W_A)
    head_of = jnp.arange(W_R) // HD_R
    seg = (head_of[:, None] == head_of[None, :]).astype(BF16)
    avg = (seg[:LANES, :LANES].astype(F32) / HD_R).astype(BF16)
    zpad = jnp.zeros((D_DECAY_LORA, W_R), F32)
    rw = {
        "mu": rw_mu[l].reshape(1, C_R), "w0": rw_w0[l].reshape(1, W_R),
        "w2": jnp.concatenate([rw_w2[l], zpad], axis=0),
        "a0": rw_a0[l].reshape(1, W_R),
        "a2": jnp.concatenate([zpad, rw_a2[l]], axis=0).astype(BF16),
        "g2": rw_g2[l].astype(BF16),
        "kk": rw_kk[l].reshape(1, W_R), "ka": rw_ka[l].reshape(1, W_R),
        "rk": rw_rk[l].reshape(1, W_R), "seg": seg,
    }
    lnw, lnb = rw_lnw[l].reshape(1, W_R), rw_lnb[l].reshape(1, W_R)
    wr = jnp.zeros((D_MODEL, ROUTE_W), F32)
    wr = wr.at[:, :N_GROUPS].set(moe_wg[l]).at[:, EXPERT_LANE0:EXPERT_LANE0 + N_EXPERTS].set(moe_we[l])
    br = jnp.zeros((1, ROUTE_W), F32)
    br = br.at[0, :N_GROUPS].set(moe_bg[l]).at[0, EXPERT_LANE0:EXPERT_LANE0 + N_EXPERTS].set(moe_be[l])
    wg_bf, wu_bf, wd_bf = (a[l].astype(BF16) for a in (moe_w_gate, moe_w_up, moe_w_down))
    g1 = norm_mix_g[l].reshape(1, D_MODEL)
    g2 = norm_ffn_g[l].reshape(1, D_MODEL)
    gf = final_g.reshape(1, D_MODEL)

    r_pad = -(bs + bp) % 8
    c_all = jnp.concatenate([c_sample, c_prompt, jnp.zeros((r_pad, D_MODEL), F32)], axis=0)
    mods = _ada_mod(c_all, ada_w[l].astype(BF16), ada_b[l])

    def mod_rows(sub, part, prompt):
        m = mods[sub, :, part * D_MODEL:(part + 1) * D_MODEL]
        if prompt:
            return m[bs:bs + 1]
        return jnp.repeat(m[:bs], ts, axis=0)

    xp = x_prompt.reshape(n_p, D_MODEL)
    qt, kb, vt, k_p, v_p, zr_p = _in_proj(
        xp, mod_rows(0, 0, True), mod_rows(0, 1, True), g1, w_bf, wt_bf, tm=512, transposed=True)
    oa_t = _attn_prompt(qt, kb, vt, g_col, lams, tq=1024, tk=512)

    tm_r = 256
    prep_p = _rwkv_prep(zr_p, None, tm_r, tm_r, rw)
    or_p, s_p = _rwkv_scan(prep_p, lnw, lnb, avg, jnp.zeros((H_R // 2, LANES, LANES), F32),
                           chunk=64, carry=True)
    mods_p = (mod_rows(0, 2, True), mod_rows(1, 0, True), mod_rows(1, 1, True), mod_rows(1, 2, True))
    y_p = _ffn(xp, oa_t, or_p, mods_p, g2, gf, wo_bf, wr, br, wg_bf, wu_bf, wd_bf,
               tm=1024, oa_transposed=True)

    xs = x_sample.reshape(n_s, D_MODEL)
    q_s, _, _, k_s, v_s, zr_s = _in_proj(
        xs, mod_rows(0, 0, False), mod_rows(0, 1, False), g1, w_bf, wt_bf, tm=256, transposed=False)
    q5 = q_s.reshape(bs, ts, 2 * H_A, DK_A)
    eye_hm = jnp.eye(2 * H_A, dtype=BF16)
    qbd = q5.transpose(0, 2, 1, 3)[:, :, :, None, :] * eye_hm[None, :, None, :, None]
    qbd = qbd.reshape(bs, 2 * H_A * ts, C_QA)
    n_pool = cache_k.shape[1]
    kt_pages = cache_k[l].transpose(0, 2, 3, 4, 1).reshape(n_pool, C_QA, PAGE_SIZE)
    vt_pages = cache_v[l].transpose(0, 2, 3, 1).reshape(n_pool, W_A, PAGE_SIZE)
    oa_s = _attn_sample(page_table, qbd, k_s.reshape(bs, ts, C_QA), v_s.reshape(bs, ts, W_A),
                        kt_pages, vt_pages, g_row, seg, lams)

    first_s = jnp.zeros((bs, ts, C_R), F32).at[:, 0].set(state_shift[l]).reshape(n_s, C_R)
    prep_s = _rwkv_prep(zr_s, first_s, ts, 256, rw)
    or_s, s_s = _rwkv_scan(prep_s, lnw, lnb, avg, _block_diag_state(state_wkv[l]),
                           chunk=ts, carry=False)
    mods_s = (mod_rows(0, 2, False), mod_rows(1, 0, False), mod_rows(1, 1, False), mod_rows(1, 2, False))
    y_s = _ffn(xs, oa_s.reshape(n_s, W_A), or_s, mods_s, g2, gf, wo_bf, wr, br, wg_bf, wu_bf, wd_bf,
               tm=512, oa_transposed=False)

    return (
        y_p.reshape(bp, sp, D_MODEL),
        y_s.reshape(bs, ts, D_MODEL),
        k_p.reshape(1, bp, sp, H_A, 2, DK_A),
        v_p.reshape(1, bp, sp, H_A, DV_A),
        _diag_state(s_p[None]).astype(state_wkv.dtype)[None],
        zr_p[n_p - 1:].reshape(1, bp, C_R),
        k_s.reshape(1, bs, ts, H_A, 2, DK_A),
        v_s.reshape(1, bs, ts, H_A, DV_A),
        _diag_state(s_s).astype(state_wkv.dtype)[None],
        zr_s.reshape(bs, ts, C_R)[:, -1][None],
    )
```
